```python
import math
import jax, jax.numpy as jnp
from jax import lax
import numpy as np

D_MODEL = 1024
BATCH = 2
SEQ = 16384
DEPTH = 4

N_GROUPS_A = 4
GROUP_DIM_A = 64
W_A = N_GROUPS_A * GROUP_DIM_A
CONV_WIDTH = 3
N_HEADS_B = 4
HEAD_DIM_B = 64
W_B = N_HEADS_B * HEAD_DIM_B
Q_BLOCK = 128
K_BLOCK = 128
N_GROUPS_C = 4
GROUP_DIM_C = 64
W_C = N_GROUPS_C * GROUP_DIM_C
CHUNK = 128
N_BRANCH = 3
BRANCH_W = W_A
SPLIT_SIZES = (W_A, W_A, W_A, W_B, W_B, W_B, W_C, W_C, N_BRANCH * D_MODEL)
IN_COLS = 3 * W_A + 3 * W_B + 2 * W_C + N_BRANCH * D_MODEL
N_EXPERTS = 16
N_EXPERT_GROUPS = 4
EXPERTS_PER_GROUP = N_EXPERTS // N_EXPERT_GROUPS
TOP_K = 2
D_FF_EXPERT = 128
DEEPNORM_ALPHA = (2 * DEPTH) ** 0.25
DEEPNORM_BETA = (8 * DEPTH) ** -0.25
LN_EPS = 1e-5

kernel_name = "hybrid_conv_stickbreak_gmlp_grouped_moe_deepnorm"


def layer_norm(x, g, b):
    xf = x.astype(jnp.float32)
    mu = jnp.mean(xf, axis=-1, keepdims=True)
    var = jnp.mean(jnp.square(xf - mu), axis=-1, keepdims=True)
    y = (xf - mu) * lax.rsqrt(var + LN_EPS) * g.astype(jnp.float32) + b.astype(jnp.float32)
    return y.astype(x.dtype)


def short_conv_mixer(h, bg, cg, conv_w):
    u = cg * h
    y = lax.conv_general_dilated(
        u, conv_w[:, None, :], window_strides=(1,),
        padding=[(CONV_WIDTH - 1, 0)],
        dimension_numbers=("NWC", "WIO", "NWC"),
        feature_group_count=W_A)
    return bg * y


def stick_breaking_attention(q, k, v):
    b, s, h, dh = q.shape
    nb = s // Q_BLOCK
    scale = dh ** -0.5
    tri_in = (jnp.arange(K_BLOCK)[:, None] > jnp.arange(K_BLOCK)[None, :]).astype(jnp.float32)
    outs = []
    for i in range(nb):
        kv_len = (i + 1) * Q_BLOCK
        nk = kv_len // K_BLOCK
        q_blk = q[:, i * Q_BLOCK:(i + 1) * Q_BLOCK]
        z = jnp.einsum("bqhd,bkhd->bhqk", q_blk, k[:, :kv_len],
                       preferred_element_type=jnp.float32) * scale
        q_pos = i * Q_BLOCK + jnp.arange(Q_BLOCK)
        strict = jnp.arange(kv_len)[None, :] < q_pos[:, None]
        log_not = jnp.where(strict, -jax.nn.softplus(z), 0.0)
        ln_blk = log_not.reshape(b, h, Q_BLOCK, nk, K_BLOCK)
        within = jnp.einsum("bhqmj,js->bhqms", ln_blk, tri_in)
        tri_blk = (jnp.arange(nk)[:, None] > jnp.arange(nk)[None, :]).astype(jnp.float32)
        blk_after = jnp.einsum("bhqn,nm->bhqm", ln_blk.sum(-1), tri_blk)
        after = (within + blk_after[..., None]).reshape(b, h, Q_BLOCK, kv_len)
        a = jnp.where(strict, jnp.exp(jax.nn.log_sigmoid(z) + after), 0.0)
        outs.append(jnp.einsum("bhqk,bkhd->bqhd", a.astype(v.dtype), v[:, :kv_len]))
    out = jnp.concatenate(outs, axis=1)
    return out.reshape(b, s, h * dh)


def spatial_gating_mixer(u, v, w_s, b_s, ln_g, ln_b):
    b, s, _ = u.shape
    v = layer_norm(v, ln_g, ln_b)
    vc = v.reshape(b, s // CHUNK, CHUNK, N_GROUPS_C, GROUP_DIM_C)
    w = w_s * jnp.tril(jnp.ones((CHUNK, CHUNK), w_s.dtype))
    f = jnp.einsum("gts,bnsgc->bntgc", w, vc) + b_s.T[None, None, :, :, None]
    return u * f.reshape(b, s, W_C)


def hybrid_mixer(x, w_in, b_gate, conv_w, sg_w, sg_b, sg_ln_g, sg_ln_b, w_br, w_o):
    b, s, d = x.shape
    proj = x @ w_in
    offsets = np.cumsum(SPLIT_SIZES)[:-1].tolist()
    h_a, bg_a, cg_a, q, k, v, u_c, v_c, gate_logits = jnp.split(proj, offsets, axis=-1)
    y_a = short_conv_mixer(h_a, bg_a, cg_a, conv_w)
    y_b = stick_breaking_attention(q.reshape(b, s, N_HEADS_B, HEAD_DIM_B),
                                   k.reshape(b, s, N_HEADS_B, HEAD_DIM_B),
                                   v.reshape(b, s, N_HEADS_B, HEAD_DIM_B))
    y_c = spatial_gating_mixer(jax.nn.gelu(u_c), jax.nn.gelu(v_c), sg_w, sg_b, sg_ln_g, sg_ln_b)
    gates = jax.nn.sigmoid(gate_logits.reshape(b, s, N_BRANCH, d) + b_gate)
    branches = (y_a, y_b, y_c)
    merged = gates[:, :, 0] * (branches[0] @ w_br[0])
    for i in range(1, N_BRANCH):
        merged = merged + gates[:, :, i] * (branches[i] @ w_br[i])
    return merged @ w_o


def moe_ffn(x, w_router, router_bias, w_gate, w_up, w_down):
    b, s, d = x.shape
    xt = x.reshape(-1, d)
    n_tok = xt.shape[0]
    scores = jax.nn.sigmoid((xt @ w_router).astype(jnp.float32))
    sel = scores + router_bias.astype(jnp.float32)
    sel_g = sel.reshape(n_tok, N_EXPERT_GROUPS, EXPERTS_PER_GROUP)
    group_score = lax.top_k(sel_g, TOP_K)[0].sum(-1)
    best_group = jnp.argmax(group_score, axis=-1)
    group_mask = jax.nn.one_hot(best_group, N_EXPERT_GROUPS, dtype=jnp.bool_)
    masked_sel = jnp.where(group_mask[:, :, None], sel_g, -jnp.inf).reshape(n_tok, N_EXPERTS)
    _, top_idx = lax.top_k(masked_sel, TOP_K)
    top_s = jnp.take_along_axis(scores, top_idx, axis=-1)
    top_w = top_s / jnp.sum(top_s, axis=-1, keepdims=True)
    dense_w = jnp.sum(jax.nn.one_hot(top_idx, N_EXPERTS, dtype=jnp.float32) * top_w[..., None], axis=1)
    dense_w = dense_w.astype(xt.dtype)
    hidden = jax.nn.silu(jnp.einsum("td,edf->tef", xt, w_gate)) * jnp.einsum("td,edf->tef", xt, w_up)
    y = jnp.einsum("tef,efd->td", hidden * dense_w[:, :, None], w_down)
    return y.reshape(b, s, d)


def setup_inputs(seed: int = 0) -> dict:
    key = jax.random.key(seed)
    ks = jax.random.split(key, 24)
    f32 = jnp.float32
    nrm = lambda k, shape, sc: jax.random.normal(k, shape, f32) * sc
    return {
        "x": nrm(ks[0], (BATCH, SEQ, D_MODEL), 1.0),
        "ln_in_g": 1.0 + nrm(ks[1], (D_MODEL,), 0.05),
        "ln_in_b": nrm(ks[2], (D_MODEL,), 0.02),
        "w_in": nrm(ks[3], (DEPTH, D_MODEL, IN_COLS), D_MODEL ** -0.5),
        "b_gate": nrm(ks[4], (DEPTH, N_BRANCH, D_MODEL), 0.1),
        "conv_w": nrm(ks[5], (DEPTH, CONV_WIDTH, W_A), CONV_WIDTH ** -0.5),
        "sg_w": nrm(ks[6], (DEPTH, N_GROUPS_C, CHUNK, CHUNK), CHUNK ** -0.5),
        "sg_b": 1.0 + nrm(ks[7], (DEPTH, N_GROUPS_C, CHUNK), 0.1),
        "sg_ln_g": 1.0 + nrm(ks[8], (DEPTH, W_C), 0.05),
        "sg_ln_b": nrm(ks[9], (DEPTH, W_C), 0.02),
        "w_br": nrm(ks[10], (DEPTH, N_BRANCH, BRANCH_W, D_MODEL), (BRANCH_W ** -0.5) * DEEPNORM_BETA),
        "w_o": nrm(ks[11], (DEPTH, D_MODEL, D_MODEL), (D_MODEL ** -0.5) * DEEPNORM_BETA),
        "ln_mix_g": 1.0 + nrm(ks[12], (DEPTH, D_MODEL), 0.05),
        "ln_mix_b": nrm(ks[13], (DEPTH, D_MODEL), 0.02),
        "w_router": nrm(ks[14], (D_MODEL, N_EXPERTS), D_MODEL ** -0.5),
        "router_bias": nrm(ks[15], (N_EXPERTS,), 0.01),
        "w_gate": nrm(ks[16], (DEPTH, N_EXPERTS, D_MODEL, D_FF_EXPERT), D_MODEL ** -0.5),
        "w_up": nrm(ks[17], (DEPTH, N_EXPERTS, D_MODEL, D_FF_EXPERT), D_MODEL ** -0.5),
        "w_down": nrm(ks[18], (DEPTH, N_EXPERTS, D_FF_EXPERT, D_MODEL), (D_FF_EXPERT ** -0.5) * DEEPNORM_BETA),
        "ln_ffn_g": 1.0 + nrm(ks[19], (DEPTH, D_MODEL), 0.05),
        "ln_ffn_b": nrm(ks[20], (DEPTH, D_MODEL), 0.02),
    }


def reference(x, ln_in_g, ln_in_b, w_in, b_gate, conv_w, sg_w, sg_b, sg_ln_g, sg_ln_b,
              w_br, w_o, ln_mix_g, ln_mix_b, w_router, router_bias, w_gate, w_up, w_down,
              ln_ffn_g, ln_ffn_b):
    h = layer_norm(x, ln_in_g, ln_in_b)
    for l in range(DEPTH):
        mix = hybrid_mixer(h, w_in[l], b_gate[l], conv_w[l], sg_w[l], sg_b[l],
                           sg_ln_g[l], sg_ln_b[l], w_br[l], w_o[l])
        h = layer_norm(DEEPNORM_ALPHA * h + mix, ln_mix_g[l], ln_mix_b[l])
        ffn = moe_ffn(h, w_router, router_bias, w_gate[l], w_up[l], w_down[l])
        h = layer_norm(DEEPNORM_ALPHA * h + ffn, ln_ffn_g[l], ln_ffn_b[l])
    return h
```

```python
import functools

import jax
import jax.numpy as jnp
from jax import lax
from jax.experimental import pallas as pl
from jax.experimental.pallas import tpu as pltpu

D_MODEL = 1024
DEPTH = 4
BRANCH_W = 256
N_HEADS = 4
HEAD_DIM = 64
CHUNK = 128
N_GROUPS_C = 4
GROUP_DIM_C = 64
MIX_COLS = 8 * BRANCH_W
N_BRANCH = 3
N_EXPERTS = 16
EXPERTS_PER_GROUP = 4
N_EXPERT_GROUPS = 4
D_FF_EXPERT = 128
FF_ALL = N_EXPERTS * D_FF_EXPERT
DEEPNORM_ALPHA = (2 * DEPTH) ** 0.25
LN_EPS = 1e-5

LANES = 128
SUBLANES = 8
VMEM_LIMIT = 56 * 1024 * 1024

TOKEN_TILE = 512
Q_TILE = 128
K_TILE = 128
EXIT_LOGMASS = 96.0

F32 = jnp.float32
BF16 = jnp.bfloat16


def _layer_norm(x, g, b):
    mu = jnp.mean(x, axis=-1, keepdims=True)
    xc = x - mu
    var = jnp.mean(xc * xc, axis=-1, keepdims=True)
    return xc * lax.rsqrt(var + LN_EPS) * g + b


def _params(sem):
    return pltpu.CompilerParams(dimension_semantics=sem, vmem_limit_bytes=VMEM_LIMIT)


def _ln_kernel(x_ref, g_ref, b_ref, o_ref):
    o_ref[...] = _layer_norm(x_ref[...], g_ref[...], b_ref[...])


def _ln_call(x, g, b):
    t = x.shape[0]
    row = pl.BlockSpec((TOKEN_TILE, D_MODEL), lambda i: (i, 0))
    vec = pl.BlockSpec((1, D_MODEL), lambda i: (0, 0))
    return pl.pallas_call(
        _ln_kernel,
        grid=(t // TOKEN_TILE,),
        in_specs=[row, vec, vec],
        out_specs=row,
        out_shape=jax.ShapeDtypeStruct((t, D_MODEL), F32),
        compiler_params=_params(("parallel",)),
        name="ln_in",
    )(x, g.reshape(1, D_MODEL), b.reshape(1, D_MODEL))


def _shift_rows(u, prev, shift):
    rolled = pltpu.roll(u, shift, axis=0)
    head_rows = lax.broadcasted_iota(jnp.int32, (SUBLANES, u.shape[1]), 0)
    head = jnp.where(head_rows < shift, pltpu.roll(prev, shift, axis=0), rolled[:SUBLANES])
    return jnp.concatenate([head, rolled[SUBLANES:]], axis=0)


def _mixer_in_kernel(tiles_per_seq, h_ref, w_ref, convw_ref, sgw_ref, sgb_ref, lng_ref, lnb_ref,
                     ya_ref, yc_ref, q_ref, k_ref, v_ref, tail_ref):
    i = pl.program_id(0)
    proj = jnp.dot(h_ref[...].astype(BF16), w_ref[...], preferred_element_type=F32)
    col = lambda j: proj[:, j * BRANCH_W:(j + 1) * BRANCH_W]
    h_a, bg, cg, q, k, v, u_c, v_c = (col(j) for j in range(8))

    @pl.when(i % tiles_per_seq == 0)
    def _():
        tail_ref[...] = jnp.zeros_like(tail_ref)

    u = cg * h_a
    prev = tail_ref[...]
    conv = (convw_ref[0:1, :] * _shift_rows(u, prev, 2)
            + convw_ref[1:2, :] * _shift_rows(u, prev, 1)
            + convw_ref[2:3, :] * u)
    tail_ref[...] = u[u.shape[0] - SUBLANES:]
    ya_ref[...] = (bg * conv).astype(BF16)

    q_ref[...] = (q * (HEAD_DIM ** -0.5)).astype(BF16)
    k_ref[...] = k.astype(BF16)
    v_ref[...] = v.astype(BF16)

    gate_u = jax.nn.gelu(u_c)
    vn = _layer_norm(jax.nn.gelu(v_c), lng_ref[...], lnb_ref[...]).astype(BF16)
    t_idx = lax.broadcasted_iota(jnp.int32, (CHUNK, CHUNK), 0)
    s_idx = lax.broadcasted_iota(jnp.int32, (CHUNK, CHUNK), 1)
    lane_group = lax.broadcasted_iota(jnp.int32, (CHUNK, BRANCH_W), 1) // GROUP_DIM_C
    w_tril = [jnp.where(s_idx <= t_idx, sgw_ref[g], 0.0).astype(BF16) for g in range(N_GROUPS_C)]
    for c in range(u.shape[0] // CHUNK):
        rows = slice(c * CHUNK, (c + 1) * CHUNK)
        v_chunk = vn[rows]
        f = sgb_ref[...]
        for g in range(N_GROUPS_C):
            fg = jnp.dot(w_tril[g], v_chunk, preferred_element_type=F32)
            f = f + jnp.where(lane_group == g, fg, 0.0)
        yc_ref[rows, :] = (gate_u[rows] * f).astype(BF16)


def _mixer_in_call(h, w_mix, conv_w, sg_w, sg_b_full, sg_ln_g, sg_ln_b, seq_len):
    t = h.shape[0]
    row_out = pl.BlockSpec((TOKEN_TILE, BRANCH_W), lambda i: (i, 0))
    full = lambda shape: pl.BlockSpec(shape, lambda i: (0,) * len(shape))
    out_sds = jax.ShapeDtypeStruct((t, BRANCH_W), BF16)
    return pl.pallas_call(
        functools.partial(_mixer_in_kernel, seq_len // TOKEN_TILE),
        grid=(t // TOKEN_TILE,),
        in_specs=[
            pl.BlockSpec((TOKEN_TILE, D_MODEL), lambda i: (i, 0)),
            full((D_MODEL, MIX_COLS)),
            full((3, BRANCH_W)),
            full((N_GROUPS_C, CHUNK, CHUNK)),
            full((CHUNK, BRANCH_W)),
            full((1, BRANCH_W)),
            full((1, BRANCH_W)),
        ],
        out_specs=[row_out] * 5,
        out_shape=[out_sds] * 5,
        scratch_shapes=[pltpu.VMEM((SUBLANES, BRANCH_W), F32)],
        compiler_params=_params(("arbitrary",)),
        name="mixer_in",
    )(h, w_mix, conv_w, sg_w, sg_b_full, sg_ln_g, sg_ln_b)


def _attn_kernel(q_ref, k_ref, v_ref, o_ref, acc_ref, spent_ref):
    qi = pl.program_id(1)
    q = q_ref[0]
    lane_head = lax.broadcasted_iota(jnp.int32, (1, BRANCH_W), 1) // HEAD_DIM
    q_heads = [jnp.where(lane_head == h, q, jnp.zeros_like(q)) for h in range(N_HEADS)]
    row = lax.broadcasted_iota(jnp.int32, (Q_TILE, K_TILE), 0)
    col = lax.broadcasted_iota(jnp.int32, (Q_TILE, K_TILE), 1)
    j_idx = lax.broadcasted_iota(jnp.int32, (2 * K_TILE, 2 * K_TILE), 0) % K_TILE
    s_idx = lax.broadcasted_iota(jnp.int32, (2 * K_TILE, 2 * K_TILE), 1)
    suffix_op = jnp.where((s_idx >= K_TILE) | (j_idx > s_idx), 1.0, 0.0).astype(BF16)

    acc_ref[...] = jnp.zeros_like(acc_ref)
    spent_ref[...] = jnp.zeros_like(spent_ref)

    def visit(state):
        kb, _ = state
        start = pl.multiple_of(kb * K_TILE, K_TILE)
        k_blk = k_ref[0, pl.ds(start, K_TILE), :]
        v_blk = v_ref[0, pl.ds(start, K_TILE), :]
        strict = (start + col) < (qi * Q_TILE + row)
        weights = []
        least = None
        for h in range(N_HEADS):
            z = lax.dot_general(q_heads[h], k_blk, (((1,), (1,)), ((), ())),
                                preferred_element_type=F32)
            softplus = jnp.maximum(z, 0.0) + jnp.log1p(jnp.exp(-jnp.abs(z)))
            softplus = jnp.where(strict, softplus, 0.0)
            hi = softplus.astype(BF16)
            lo = (softplus - hi.astype(F32)).astype(BF16)
            sums = jnp.dot(jnp.concatenate([hi, lo], axis=1), suffix_op,
                           preferred_element_type=F32)
            spent = spent_ref[h]
            later = sums[:, :K_TILE] + spent
            a = jnp.where(strict, jnp.exp(z - softplus - later), 0.0)
            weights.append(a.astype(BF16))
            spent = spent + sums[:, K_TILE:]
            spent_ref[h] = spent
            head_least = jnp.min(spent)
            least = head_least if least is None else jnp.minimum(least, head_least)
        v_heads = jnp.concatenate(
            [jnp.where(lane_head == h, v_blk, jnp.zeros_like(v_blk)) for h in range(N_HEADS)], axis=0)
        acc_ref[...] += jnp.dot(jnp.concatenate(weights, axis=1), v_heads,
                                preferred_element_type=F32)
        return kb - 1, least

    def unfinished(state):
        kb, least = state
        return (kb >= 0) & (least < EXIT_LOGMASS)

    lax.while_loop(unfinished, visit, (qi * Q_TILE // K_TILE, jnp.float32(0.0)))
    o_ref[0] = acc_ref[...].astype(BF16)


def _attn_call(q, k, v):
    b, s, _ = q.shape
    tile = pl.BlockSpec((1, Q_TILE, BRANCH_W), lambda bi, qi: (bi, qi, 0))
    seq = pl.BlockSpec((1, s, BRANCH_W), lambda bi, qi: (bi, 0, 0))
    return pl.pallas_call(
        _attn_kernel,
        grid=(b, s // Q_TILE),
        in_specs=[tile, seq, seq],
        out_specs=tile,
        out_shape=jax.ShapeDtypeStruct((b, s, BRANCH_W), BF16),
        scratch_shapes=[pltpu.VMEM((Q_TILE, BRANCH_W), F32),
                        pltpu.VMEM((N_HEADS, Q_TILE, K_TILE), F32)],
        compiler_params=_params(("parallel", "parallel")),
        name="attn",
    )(q, k, v)


def _merge_kernel(h_ref, ya_ref, yb_ref, yc_ref, wg_ref, bgate_ref, wbr_ref, wo_ref, g_ref, b_ref,
                  o_ref):
    h = h_ref[...]
    logits = jnp.dot(h.astype(BF16), wg_ref[...], preferred_element_type=F32)
    merged = None
    for i, y_ref in enumerate((ya_ref, yb_ref, yc_ref)):
        gate = jax.nn.sigmoid(logits[:, i * D_MODEL:(i + 1) * D_MODEL] + bgate_ref[i:i + 1, :])
        term = gate * jnp.dot(y_ref[...], wbr_ref[i], preferred_element_type=F32)
        merged = term if merged is None else merged + term
    mix = jnp.dot(merged.astype(BF16), wo_ref[...], preferred_element_type=F32)
    o_ref[...] = _layer_norm(DEEPNORM_ALPHA * h + mix, g_ref[...], b_ref[...])


def _merge_call(h, y_a, y_b, y_c, w_gate_cols, b_gate, w_br, w_o, ln_g, ln_b):
    t = h.shape[0]
    row = pl.BlockSpec((TOKEN_TILE, D_MODEL), lambda i: (i, 0))
    branch = pl.BlockSpec((TOKEN_TILE, BRANCH_W), lambda i: (i, 0))
    full = lambda shape: pl.BlockSpec(shape, lambda i: (0,) * len(shape))
    return pl.pallas_call(
        _merge_kernel,
        grid=(t // TOKEN_TILE,),
        in_specs=[row, branch, branch, branch,
                  full((D_MODEL, N_BRANCH * D_MODEL)),
                  full((N_BRANCH, D_MODEL)),
                  full((N_BRANCH, BRANCH_W, D_MODEL)),
                  full((D_MODEL, D_MODEL)),
                  full((1, D_MODEL)), full((1, D_MODEL))],
        out_specs=row,
        out_shape=jax.ShapeDtypeStruct((t, D_MODEL), F32),
        compiler_params=_params(("parallel",)),
        name="merge",
    )(h, y_a, y_b, y_c, w_gate_cols, b_gate, w_br, w_o, ln_g, ln_b)


def _first_index_of_max(x, lane):
    m = jnp.max(x, axis=1, keepdims=True)
    idx = jnp.min(jnp.where(x == m, lane, LANES), axis=1, keepdims=True)
    return m, idx


def _moe_kernel(x_ref, wr_ref, rb_ref, wg_ref, wu_ref, wd_ref, expand_ref, g_ref, b_ref, o_ref):
    x = x_ref[...]
    logits = jnp.dot(x, wr_ref[...], preferred_element_type=F32, precision=lax.Precision.HIGHEST)
    scores = jax.nn.sigmoid(logits)
    sel = scores + rb_ref[...]
    lane = lax.broadcasted_iota(jnp.int32, sel.shape, 1)
    group_of_lane = lane // EXPERTS_PER_GROUP
    best = None
    for g in range(N_EXPERT_GROUPS):
        sel_g = jnp.where(group_of_lane == g, sel, -jnp.inf)
        m1, i1 = _first_index_of_max(sel_g, lane)
        m2, i2 = _first_index_of_max(jnp.where(lane == i1, -jnp.inf, sel_g), lane)
        cand = (m1 + m2, i1, i2)
        if best is None:
            best = cand
        else:
            better = cand[0] > best[0]
            best = tuple(jnp.where(better, c, o) for c, o in zip(cand, best))
    _, i1, i2 = best
    s1 = jnp.sum(jnp.where(lane == i1, scores, 0.0), axis=1, keepdims=True)
    s2 = jnp.sum(jnp.where(lane == i2, scores, 0.0), axis=1, keepdims=True)
    denom = s1 + s2
    dense_w = jnp.where(lane == i1, s1 / denom, 0.0) + jnp.where(lane == i2, s2 / denom, 0.0)
    w_hi = dense_w.astype(BF16)
    w_lo = (dense_w - w_hi.astype(F32)).astype(BF16)
    w_cols = (jnp.dot(w_hi, expand_ref[...], preferred_element_type=F32)
              + jnp.dot(w_lo, expand_ref[...], preferred_element_type=F32))

    xb = x.astype(BF16)
    gate = jnp.dot(xb, wg_ref[...], preferred_element_type=F32)
    up = jnp.dot(xb, wu_ref[...], preferred_element_type=F32)
    hidden = jax.nn.silu(gate) * up * w_cols
    y = jnp.dot(hidden.astype(BF16), wd_ref[...], preferred_element_type=F32)
    o_ref[...] = _layer_norm(DEEPNORM_ALPHA * x + y, g_ref[...], b_ref[...])


def _moe_call(x, wr_pad, rb_pad, w_gate2, w_up2, w_down2, expand, ln_g, ln_b):
    t = x.shape[0]
    row = pl.BlockSpec((TOKEN_TILE, D_MODEL), lambda i: (i, 0))
    full = lambda shape: pl.BlockSpec(shape, lambda i: (0,) * len(shape))
    return pl.pallas_call(
        _moe_kernel,
        grid=(t // TOKEN_TILE,),
        in_specs=[row,
                  full((D_MODEL, LANES)), full((1, LANES)),
                  full((D_MODEL, FF_ALL)), full((D_MODEL, FF_ALL)), full((FF_ALL, D_MODEL)),
                  full((LANES, FF_ALL)),
                  full((1, D_MODEL)), full((1, D_MODEL))],
        out_specs=row,
        out_shape=jax.ShapeDtypeStruct((t, D_MODEL), F32),
        compiler_params=_params(("parallel",)),
        name="moe",
    )(x, wr_pad, rb_pad, w_gate2, w_up2, w_down2, expand, ln_g, ln_b)


def kernel(x, ln_in_g, ln_in_b, w_in, b_gate, conv_w, sg_w, sg_b, sg_ln_g, sg_ln_b, w_br, w_o,
           ln_mix_g, ln_mix_b, w_router, router_bias, w_gate, w_up, w_down, ln_ffn_g, ln_ffn_b):
    batch, seq_len, d = x.shape
    assert d == D_MODEL and seq_len % TOKEN_TILE == 0 and seq_len % Q_TILE == 0
    t = batch * seq_len
    vec = lambda a: a.reshape(1, -1)

    wr_pad = jnp.pad(w_router, ((0, 0), (0, LANES - N_EXPERTS)))
    rb_pad = jnp.pad(router_bias, (0, LANES - N_EXPERTS)).reshape(1, LANES)
    expand = (lax.broadcasted_iota(jnp.int32, (LANES, FF_ALL), 0)
              == lax.broadcasted_iota(jnp.int32, (LANES, FF_ALL), 1) // D_FF_EXPERT).astype(BF16)

    h = _ln_call(x.reshape(t, d), ln_in_g, ln_in_b)
    for l in range(DEPTH):
        w_mix = w_in[l, :, :MIX_COLS].astype(BF16)
        w_gate_cols = w_in[l, :, MIX_COLS:].astype(BF16)
        sg_b_full = jnp.repeat(sg_b[l].T, GROUP_DIM_C, axis=1)
        y_a, y_c, q, k, v = _mixer_in_call(h, w_mix, conv_w[l], sg_w[l], sg_b_full,
                                           vec(sg_ln_g[l]), vec(sg_ln_b[l]), seq_len)
        shape3 = (batch, seq_len, BRANCH_W)
        y_b = _attn_call(q.reshape(shape3), k.reshape(shape3), v.reshape(shape3)).reshape(t, BRANCH_W)
        h = _merge_call(h, y_a, y_b, y_c, w_gate_cols, b_gate[l], w_br[l].astype(BF16),
                        w_o[l].astype(BF16), vec(ln_mix_g[l]), vec(ln_mix_b[l]))
        w_gate2 = w_gate[l].transpose(1, 0, 2).reshape(d, FF_ALL).astype(BF16)
        w_up2 = w_up[l].transpose(1, 0, 2).reshape(d, FF_ALL).astype(BF16)
        w_down2 = w_down[l].reshape(FF_ALL, d).astype(BF16)
        h = _moe_call(h, wr_pad, rb_pad, w_gate2, w_up2, w_down2, expand,
                      vec(ln_ffn_g[l]), vec(ln_ffn_b[l]))
    return h.reshape(batch, seq_len, d)
```

```python
import functools

import jax
import jax.numpy as jnp
from jax import lax
from jax.experimental import pallas as pl
from jax.experimental.pallas import tpu as pltpu

D_MODEL = 1024
DEPTH = 4
BRANCH_W = 256
N_HEADS = 4
HEAD_DIM = 64
CHUNK = 128
N_GROUPS_C = 4
GROUP_DIM_C = 64
MIX_COLS = 8 * BRANCH_W
N_BRANCH = 3
N_EXPERTS = 16
EXPERTS_PER_GROUP = 4
N_EXPERT_GROUPS = 4
D_FF_EXPERT = 128
FF_ALL = N_EXPERTS * D_FF_EXPERT
DEEPNORM_ALPHA = (2 * DEPTH) ** 0.25
LN_EPS = 1e-5

LANES = 128
SUBLANES = 8
VMEM_LIMIT = 56 * 1024 * 1024

TOKEN_TILE = 512
Q_TILE = 128
K_TILE = 128
Q_SUBTILES = 2
assert Q_TILE == K_TILE
EXIT_BITS = 127.0
NO_TILE_BITS = 16384.0
LOG2_E = 1.4426950408889634

F32 = jnp.float32
BF16 = jnp.bfloat16


def _layer_norm(x, g, b):
    mu = jnp.mean(x, axis=-1, keepdims=True)
    xc = x - mu
    var = jnp.mean(xc * xc, axis=-1, keepdims=True)
    return xc * lax.rsqrt(var + LN_EPS) * g + b


def _params(sem):
    return pltpu.CompilerParams(dimension_semantics=sem, vmem_limit_bytes=VMEM_LIMIT)


def _ln_kernel(x_ref, g_ref, b_ref, o_ref):
    o_ref[...] = _layer_norm(x_ref[...], g_ref[...], b_ref[...])


def _ln_call(x, g, b):
    t = x.shape[0]
    row = pl.BlockSpec((TOKEN_TILE, D_MODEL), lambda i: (i, 0))
    vec = pl.BlockSpec((1, D_MODEL), lambda i: (0, 0))
    return pl.pallas_call(
        _ln_kernel,
        grid=(t // TOKEN_TILE,),
        in_specs=[row, vec, vec],
        out_specs=row,
        out_shape=jax.ShapeDtypeStruct((t, D_MODEL), F32),
        compiler_params=_params(("parallel",)),
        name="ln_in",
    )(x, g.reshape(1, D_MODEL), b.reshape(1, D_MODEL))


def _shift_rows(u, prev, shift):
    rolled = pltpu.roll(u, shift, axis=0)
    head_rows = lax.broadcasted_iota(jnp.int32, (SUBLANES, u.shape[1]), 0)
    head = jnp.where(head_rows < shift, pltpu.roll(prev, shift, axis=0), rolled[:SUBLANES])
    return jnp.concatenate([head, rolled[SUBLANES:]], axis=0)


def _mixer_in_kernel(tiles_per_seq, h_ref, w_ref, convw_ref, sgw_ref, sgb_ref, lng_ref, lnb_ref,
                     ya_ref, yc_ref, q_ref, k_ref, v_ref, tail_ref):
    @pl.when(pl.program_id(0) % tiles_per_seq == 0)
    def _():
        tail_ref[...] = jnp.zeros_like(tail_ref)

    hb = h_ref[...].astype(BF16)
    proj_c = jnp.dot(hb, w_ref[:, 6 * BRANCH_W:], preferred_element_type=F32)
    proj_a = jnp.dot(hb, w_ref[:, :3 * BRANCH_W], preferred_element_type=F32)
    proj_b = jnp.dot(hb, w_ref[:, 3 * BRANCH_W:6 * BRANCH_W], preferred_element_type=F32)
    col = lambda p, j: p[:, j * BRANCH_W:(j + 1) * BRANCH_W]

    gate_u = jax.nn.gelu(col(proj_c, 0))
    vn = _layer_norm(jax.nn.gelu(col(proj_c, 1)), lng_ref[...], lnb_ref[...]).astype(BF16)
    t_idx = lax.broadcasted_iota(jnp.int32, (CHUNK, CHUNK), 0)
    s_idx = lax.broadcasted_iota(jnp.int32, (CHUNK, CHUNK), 1)
    w_cat = jnp.concatenate(
        [jnp.where(s_idx <= t_idx, sgw_ref[g], 0.0).astype(BF16) for g in range(N_GROUPS_C)], axis=1)
    lane_group = lax.broadcasted_iota(jnp.int32, (1, BRANCH_W), 1) // GROUP_DIM_C
    for c in range(hb.shape[0] // CHUNK):
        rows = slice(c * CHUNK, (c + 1) * CHUNK)
        v_chunk = vn[rows]
        v_groups = jnp.concatenate(
            [jnp.where(lane_group == g, v_chunk, jnp.zeros_like(v_chunk)) for g in range(N_GROUPS_C)],
            axis=0)
        f = jnp.dot(w_cat, v_groups, preferred_element_type=F32) + sgb_ref[...]
        yc_ref[rows, :] = (gate_u[rows] * f).astype(BF16)

    u = col(proj_a, 2) * col(proj_a, 0)
    prev = tail_ref[...]
    conv = (convw_ref[0:1, :] * _shift_rows(u, prev, 2)
            + convw_ref[1:2, :] * _shift_rows(u, prev, 1)
            + convw_ref[2:3, :] * u)
    tail_ref[...] = u[u.shape[0] - SUBLANES:]
    ya_ref[...] = (col(proj_a, 1) * conv).astype(BF16)

    q_ref[...] = (col(proj_b, 0) * (HEAD_DIM ** -0.5 * LOG2_E)).astype(BF16)
    k_ref[...] = col(proj_b, 1).astype(BF16)
    v_ref[...] = col(proj_b, 2).astype(BF16)


def _mixer_in_call(h, w_mix, conv_w, sg_w, sg_b_full, sg_ln_g, sg_ln_b, seq_len):
    t = h.shape[0]
    row_out = pl.BlockSpec((TOKEN_TILE, BRANCH_W), lambda i: (i, 0))
    full = lambda shape: pl.BlockSpec(shape, lambda i: (0,) * len(shape))
    out_sds = jax.ShapeDtypeStruct((t, BRANCH_W), BF16)
    return pl.pallas_call(
        functools.partial(_mixer_in_kernel, seq_len // TOKEN_TILE),
        grid=(t // TOKEN_TILE,),
        in_specs=[
            pl.BlockSpec((TOKEN_TILE, D_MODEL), lambda i: (i, 0)),
            full((D_MODEL, MIX_COLS)),
            full((3, BRANCH_W)),
            full((N_GROUPS_C, CHUNK, CHUNK)),
            full((CHUNK, BRANCH_W)),
            full((1, BRANCH_W)),
            full((1, BRANCH_W)),
        ],
        out_specs=[row_out] * 5,
        out_shape=[out_sds] * 5,
        scratch_shapes=[pltpu.VMEM((SUBLANES, BRANCH_W), F32)],
        compiler_params=_params(("arbitrary",)),
        name="mixer_in",
    )(h, w_mix, conv_w, sg_w, sg_b_full, sg_ln_g, sg_ln_b)


def _attn_kernel(q_ref, k_ref, v_ref, o_ref, acc_ref, spent_ref):
    lane_head = lax.broadcasted_iota(jnp.int32, (1, BRANCH_W), 1) // HEAD_DIM
    t_idx = lax.broadcasted_iota(jnp.int32, (N_HEADS * Q_TILE, K_TILE), 0) % Q_TILE
    s_idx = lax.broadcasted_iota(jnp.int32, (N_HEADS * Q_TILE, K_TILE), 1)
    below_diagonal = s_idx < t_idx
    j_op = lax.broadcasted_iota(jnp.int32, (2 * K_TILE, 2 * K_TILE), 0) % K_TILE
    s_op = lax.broadcasted_iota(jnp.int32, (2 * K_TILE, 2 * K_TILE), 1)
    suffix_op = jnp.where((s_op >= K_TILE) | (j_op > s_op), 1.0, 0.0).astype(BF16)

    def visit(q_stack, kb, spent, diagonal):
        start = pl.multiple_of(jnp.maximum(kb, 0) * K_TILE, K_TILE)
        k_blk = k_ref[0, pl.ds(start, K_TILE), :]
        v_blk = v_ref[0, pl.ds(start, K_TILE), :]
        z = lax.dot_general(q_stack, k_blk, (((1,), (1,)), ((), ())), preferred_element_type=F32)
        softplus = jnp.maximum(z, 0.0) + jnp.log2(1.0 + jnp.exp2(-jnp.abs(z)))
        if diagonal:
            softplus = jnp.where(below_diagonal, softplus, 0.0)
        else:
            spent = spent + jnp.where(kb >= 0, 0.0, NO_TILE_BITS)
        hi = softplus.astype(BF16)
        lo = (softplus - hi.astype(F32)).astype(BF16)
        sums = jnp.dot(jnp.concatenate([hi, lo], axis=1), suffix_op, preferred_element_type=F32)
        a = jnp.exp2(z - softplus - (sums[:, :K_TILE] + spent))
        if diagonal:
            a = jnp.where(below_diagonal, a, 0.0)
        a = a.astype(BF16)
        a_heads = jnp.concatenate([a[h * Q_TILE:(h + 1) * Q_TILE] for h in range(N_HEADS)], axis=1)
        v_heads = jnp.concatenate(
            [jnp.where(lane_head == h, v_blk, jnp.zeros_like(v_blk)) for h in range(N_HEADS)], axis=0)
        out = jnp.dot(a_heads, v_heads, preferred_element_type=F32)
        return spent + sums[:, K_TILE:], out

    def stacked_queries(sub):
        q = q_ref[0, sub * Q_TILE:(sub + 1) * Q_TILE, :]
        return jnp.concatenate(
            [jnp.where(lane_head == h, q, jnp.zeros_like(q)) for h in range(N_HEADS)], axis=0)

    first_tile = pl.program_id(1) * Q_SUBTILES
    least = None
    for sub in range(Q_SUBTILES):
        tile = first_tile + sub
        q_stack = stacked_queries(sub)
        spent, out0 = visit(q_stack, tile, jnp.zeros((N_HEADS * Q_TILE, K_TILE), F32), True)
        spent, out1 = visit(q_stack, tile - 1, spent, False)
        acc_ref[sub] = out0 + out1
        spent_ref[sub] = spent
        sub_least = jnp.min(spent)
        least = sub_least if least is None else jnp.minimum(least, sub_least)

    def walk(state):
        step, _ = state
        least = None
        for sub in range(Q_SUBTILES):
            spent, out = visit(stacked_queries(sub), first_tile + sub - step, spent_ref[sub], False)
            acc_ref[sub] += out
            spent_ref[sub] = spent
            sub_least = jnp.min(spent)
            least = sub_least if least is None else jnp.minimum(least, sub_least)
        return step + 1, least

    def unfinished(state):
        _, least = state
        return least < EXIT_BITS

    lax.while_loop(unfinished, walk, (jnp.int32(2), least))
    for sub in range(Q_SUBTILES):
        o_ref[0, sub * Q_TILE:(sub + 1) * Q_TILE, :] = acc_ref[sub].astype(BF16)


def _attn_call(q, k, v):
    b, s, _ = q.shape
    rows = Q_SUBTILES * Q_TILE
    tile = pl.BlockSpec((1, rows, BRANCH_W), lambda bi, qi: (bi, qi, 0))
    seq = pl.BlockSpec((1, s, BRANCH_W), lambda bi, qi: (bi, 0, 0))
    return pl.pallas_call(
        _attn_kernel,
        grid=(b, s // rows),
        in_specs=[tile, seq, seq],
        out_specs=tile,
        out_shape=jax.ShapeDtypeStruct((b, s, BRANCH_W), BF16),
        scratch_shapes=[pltpu.VMEM((Q_SUBTILES, Q_TILE, BRANCH_W), F32),
                        pltpu.VMEM((Q_SUBTILES, N_HEADS * Q_TILE, K_TILE), F32)],
        compiler_params=_params(("parallel", "parallel")),
        name="attn",
    )(q, k, v)


def _merge_kernel(h_ref, ya_ref, yb_ref, yc_ref, wg_ref, bgate_ref, wbr_ref, wo_ref, g_ref, b_ref,
                  o_ref):
    h = h_ref[...]
    logits = jnp.dot(h.astype(BF16), wg_ref[...], preferred_element_type=F32)
    merged = None
    for i, y_ref in enumerate((ya_ref, yb_ref, yc_ref)):
        gate = jax.nn.sigmoid(logits[:, i * D_MODEL:(i + 1) * D_MODEL] + bgate_ref[i:i + 1, :])
        term = gate * jnp.dot(y_ref[...], wbr_ref[i], preferred_element_type=F32)
        merged = term if merged is None else merged + term
    mix = jnp.dot(merged.astype(BF16), wo_ref[...], preferred_element_type=F32)
    o_ref[...] = _layer_norm(DEEPNORM_ALPHA * h + mix, g_ref[...], b_ref[...])


def _merge_call(h, y_a, y_b, y_c, w_gate_cols, b_gate, w_br, w_o, ln_g, ln_b):
    t = h.shape[0]
    row = pl.BlockSpec((TOKEN_TILE, D_MODEL), lambda i: (i, 0))
    branch = pl.BlockSpec((TOKEN_TILE, BRANCH_W), lambda i: (i, 0))
    full = lambda shape: pl.BlockSpec(shape, lambda i: (0,) * len(shape))
    return pl.pallas_call(
        _merge_kernel,
        grid=(t // TOKEN_TILE,),
        in_specs=[row, branch, branch, branch,
                  full((D_MODEL, N_BRANCH * D_MODEL)),
                  full((N_BRANCH, D_MODEL)),
                  full((N_BRANCH, BRANCH_W, D_MODEL)),
                  full((D_MODEL, D_MODEL)),
                  full((1, D_MODEL)), full((1, D_MODEL))],
        out_specs=row,
        out_shape=jax.ShapeDtypeStruct((t, D_MODEL), F32),
        compiler_params=_params(("parallel",)),
        name="merge",
    )(h, y_a, y_b, y_c, w_gate_cols, b_gate, w_br, w_o, ln_g, ln_b)


def _first_index_of_max(x, lane):
    m = jnp.max(x, axis=1, keepdims=True)
    idx = jnp.min(jnp.where(x == m, lane, LANES), axis=1, keepdims=True)
    return m, idx


def _moe_kernel(x_ref, wr_ref, rb_ref, wg_ref, wu_ref, wd_ref, g_ref, b_ref, o_ref):
    x = x_ref[...]
    xb = x.astype(BF16)
    x_lo = (x - xb.astype(F32)).astype(BF16)
    r = jnp.dot(xb, wr_ref[...], preferred_element_type=F32)
    logits = (r[:, :LANES] + r[:, LANES:]
              + jnp.dot(x_lo, wr_ref[:, :LANES], preferred_element_type=F32))
    scores = jax.nn.sigmoid(logits)
    sel = scores + rb_ref[...]
    lane = lax.broadcasted_iota(jnp.int32, sel.shape, 1)
    group_of_lane = lane // EXPERTS_PER_GROUP
    best = None
    for g in range(N_EXPERT_GROUPS):
        sel_g = jnp.where(group_of_lane == g, sel, -jnp.inf)
        m1, i1 = _first_index_of_max(sel_g, lane)
        m2, i2 = _first_index_of_max(jnp.where(lane == i1, -jnp.inf, sel_g), lane)
        cand = (m1 + m2, i1, i2)
        if best is None:
            best = cand
        else:
            better = cand[0] > best[0]
            best = tuple(jnp.where(better, c, o) for c, o in zip(cand, best))
    _, i1, i2 = best
    s1 = jnp.sum(jnp.where(lane == i1, scores, 0.0), axis=1, keepdims=True)
    s2 = jnp.sum(jnp.where(lane == i2, scores, 0.0), axis=1, keepdims=True)
    denom = s1 + s2
    dense_w = jnp.where(lane == i1, s1 / denom, 0.0) + jnp.where(lane == i2, s2 / denom, 0.0)

    group_cols = EXPERTS_PER_GROUP * D_FF_EXPERT
    y = None
    for g in range(N_EXPERT_GROUPS):
        cols = slice(g * group_cols, (g + 1) * group_cols)
        gate = jnp.dot(xb, wg_ref[:, cols], preferred_element_type=F32)
        up = jnp.dot(xb, wu_ref[:, cols], preferred_element_type=F32)
        w_cols = jnp.concatenate(
            [jnp.broadcast_to(dense_w[:, e:e + 1], (x.shape[0], D_FF_EXPERT))
             for e in range(g * EXPERTS_PER_GROUP, (g + 1) * EXPERTS_PER_GROUP)], axis=1)
        hidden = (jax.nn.silu(gate) * up * w_cols).astype(BF16)
        part = jnp.dot(hidden, wd_ref[cols, :], preferred_element_type=F32)
        y = part if y is None else y + part
    o_ref[...] = _layer_norm(DEEPNORM_ALPHA * x + y, g_ref[...], b_ref[...])


def _moe_call(x, wr_split, rb_pad, w_gate2, w_up2, w_down2, ln_g, ln_b):
    t = x.shape[0]
    row = pl.BlockSpec((TOKEN_TILE, D_MODEL), lambda i: (i, 0))
    full = lambda shape: pl.BlockSpec(shape, lambda i: (0,) * len(shape))
    return pl.pallas_call(
        _moe_kernel,
        grid=(t // TOKEN_TILE,),
        in_specs=[row,
                  full((D_MODEL, 2 * LANES)), full((1, LANES)),
                  full((D_MODEL, FF_ALL)), full((D_MODEL, FF_ALL)), full((FF_ALL, D_MODEL)),
                  full((1, D_MODEL)), full((1, D_MODEL))],
        out_specs=row,
        out_shape=jax.ShapeDtypeStruct((t, D_MODEL), F32),
        compiler_params=_params(("parallel",)),
        name="moe",
    )(x, wr_split, rb_pad, w_gate2, w_up2, w_down2, ln_g, ln_b)


def kernel(x, ln_in_g, ln_in_b, w_in, b_gate, conv_w, sg_w, sg_b, sg_ln_g, sg_ln_b, w_br, w_o,
           ln_mix_g, ln_mix_b, w_router, router_bias, w_gate, w_up, w_down, ln_ffn_g, ln_ffn_b):
    batch, seq_len, d = x.shape
    assert d == D_MODEL and seq_len % TOKEN_TILE == 0 and seq_len % (Q_SUBTILES * Q_TILE) == 0
    t = batch * seq_len
    vec = lambda a: a.reshape(1, -1)

    wr_pad = jnp.pad(w_router, ((0, 0), (0, LANES - N_EXPERTS)))
    wr_hi = wr_pad.astype(BF16)
    wr_split = jnp.concatenate([wr_hi, (wr_pad - wr_hi.astype(F32)).astype(BF16)], axis=1)
    rb_pad = jnp.pad(router_bias, (0, LANES - N_EXPERTS)).reshape(1, LANES)

    h = _ln_call(x.reshape(t, d), ln_in_g, ln_in_b)
    for l in range(DEPTH):
        w_mix = w_in[l, :, :MIX_COLS].astype(BF16)
        w_gate_cols = w_in[l, :, MIX_COLS:].astype(BF16)
        sg_b_full = jnp.repeat(sg_b[l].T, GROUP_DIM_C, axis=1)
        y_a, y_c, q, k, v = _mixer_in_call(h, w_mix, conv_w[l], sg_w[l], sg_b_full,
                                           vec(sg_ln_g[l]), vec(sg_ln_b[l]), seq_len)
        shape3 = (batch, seq_len, BRANCH_W)
        y_b = _attn_call(q.reshape(shape3), k.reshape(shape3), v.reshape(shape3)).reshape(t, BRANCH_W)
        h = _merge_call(h, y_a, y_b, y_c, w_gate_cols, b_gate[l], w_br[l].astype(BF16),
                        w_o[l].astype(BF16), vec(ln_mix_g[l]), vec(ln_mix_b[l]))
        w_gate2 = w_gate[l].transpose(1, 0, 2).reshape(d, FF_ALL).astype(BF16)
        w_up2 = w_up[l].transpose(1, 0, 2).reshape(d, FF_ALL).astype(BF16)
        w_down2 = w_down[l].reshape(FF_ALL, d).astype(BF16)
        h = _moe_call(h, wr_split, rb_pad, w_gate2, w_up2, w_down2,
                      vec(ln_ffn_g[l]), vec(ln_ffn_b[l]))
    return h.reshape(batch, seq_len, d)
```

```python
import functools

import jax
import jax.numpy as jnp
from jax import lax
from jax.experimental import pallas as pl
from jax.experimental.pallas import tpu as pltpu

D_MODEL = 1024
DEPTH = 4
BRANCH_W = 256
N_HEADS = 4
HEAD_DIM = 64
CHUNK = 128
N_GROUPS_C = 4
GROUP_DIM_C = 64
MIX_COLS = 8 * BRANCH_W
N_BRANCH = 3
N_EXPERTS = 16
EXPERTS_PER_GROUP = 4
N_EXPERT_GROUPS = 4
D_FF_EXPERT = 128
FF_ALL = N_EXPERTS * D_FF_EXPERT
DEEPNORM_ALPHA = (2 * DEPTH) ** 0.25
LN_EPS = 1e-5

LANES = 128
SUBLANES = 8
VMEM_LIMIT = 56 * 1024 * 1024

TOKEN_TILE = 512
CHANNEL_TILE = 1024
Q_TILE = 128
K_TILE = 128
Q_SUBTILES = 2
assert Q_TILE == K_TILE
EXIT_BITS = 127.0
NO_TILE_BITS = 16384.0
LOG2_E = 1.4426950408889634

F32 = jnp.float32
BF16 = jnp.bfloat16


def _layer_norm(x, g, b):
    mu = jnp.mean(x, axis=-1, keepdims=True)
    xc = x - mu
    var = jnp.mean(xc * xc, axis=-1, keepdims=True)
    return xc * lax.rsqrt(var + LN_EPS) * g + b


def _slabs(n_rows):
    return [slice(r, r + TOKEN_TILE) for r in range(0, n_rows, TOKEN_TILE)]


def _params(sem):
    return pltpu.CompilerParams(dimension_semantics=sem, vmem_limit_bytes=VMEM_LIMIT)


def _shift_rows(u, prev, shift):
    rolled = pltpu.roll(u, shift, axis=0)
    head_rows = lax.broadcasted_iota(jnp.int32, (SUBLANES, u.shape[1]), 0)
    head = jnp.where(head_rows < shift, pltpu.roll(prev, shift, axis=0), rolled[:SUBLANES])
    return jnp.concatenate([head, rolled[SUBLANES:]], axis=0)


def _mixer_in_kernel(tiles_per_seq, normalize_input, *refs):
    if normalize_input:
        (h_ref, ing_ref, inb_ref, w_ref, convw_ref, sgw_ref, sgb_ref, lng_ref, lnb_ref,
         h_out_ref, ya_ref, yc_ref, q_ref, k_ref, v_ref, tail_ref) = refs
    else:
        (h_ref, w_ref, convw_ref, sgw_ref, sgb_ref, lng_ref, lnb_ref,
         ya_ref, yc_ref, q_ref, k_ref, v_ref, tail_ref) = refs

    @pl.when(pl.program_id(0) % tiles_per_seq == 0)
    def _():
        tail_ref[...] = jnp.zeros_like(tail_ref)

    h = h_ref[...]
    if normalize_input:
        h = _layer_norm(h, ing_ref[...], inb_ref[...])
        h_out_ref[...] = h
    hb = h.astype(BF16)
    proj_c = jnp.dot(hb, w_ref[:, 6 * BRANCH_W:], preferred_element_type=F32)
    proj_a = jnp.dot(hb, w_ref[:, :3 * BRANCH_W], preferred_element_type=F32)
    proj_b = jnp.dot(hb, w_ref[:, 3 * BRANCH_W:6 * BRANCH_W], preferred_element_type=F32)
    col = lambda p, j: p[:, j * BRANCH_W:(j + 1) * BRANCH_W]

    gate_u = jax.nn.gelu(col(proj_c, 0))
    vn = _layer_norm(jax.nn.gelu(col(proj_c, 1)), lng_ref[...], lnb_ref[...]).astype(BF16)
    t_idx = lax.broadcasted_iota(jnp.int32, (CHUNK, CHUNK), 0)
    s_idx = lax.broadcasted_iota(jnp.int32, (CHUNK, CHUNK), 1)
    w_cat = jnp.concatenate(
        [jnp.where(s_idx <= t_idx, sgw_ref[g], 0.0).astype(BF16) for g in range(N_GROUPS_C)], axis=1)
    lane_group = lax.broadcasted_iota(jnp.int32, (1, BRANCH_W), 1) // GROUP_DIM_C
    for c in range(hb.shape[0] // CHUNK):
        rows = slice(c * CHUNK, (c + 1) * CHUNK)
        v_chunk = vn[rows]
        v_groups = jnp.concatenate(
            [jnp.where(lane_group == g, v_chunk, jnp.zeros_like(v_chunk)) for g in range(N_GROUPS_C)],
            axis=0)
        f = jnp.dot(w_cat, v_groups, preferred_element_type=F32) + sgb_ref[...]
        yc_ref[rows, :] = (gate_u[rows] * f).astype(BF16)

    u = col(proj_a, 2) * col(proj_a, 0)
    prev = tail_ref[...]
    conv = (convw_ref[0:1, :] * _shift_rows(u, prev, 2)
            + convw_ref[1:2, :] * _shift_rows(u, prev, 1)
            + convw_ref[2:3, :] * u)
    tail_ref[...] = u[u.shape[0] - SUBLANES:]
    ya_ref[...] = (col(proj_a, 1) * conv).astype(BF16)

    q_ref[...] = (col(proj_b, 0) * (HEAD_DIM ** -0.5 * LOG2_E)).astype(BF16)
    k_ref[...] = col(proj_b, 1).astype(BF16)
    v_ref[...] = col(proj_b, 2).astype(BF16)


def _mixer_in_call(h, w_mix, conv_w, sg_w, sg_b_full, sg_ln_g, sg_ln_b, seq_len, input_ln=None):
    t = h.shape[0]
    row = pl.BlockSpec((TOKEN_TILE, D_MODEL), lambda i: (i, 0))
    row_out = pl.BlockSpec((TOKEN_TILE, BRANCH_W), lambda i: (i, 0))
    full = lambda shape: pl.BlockSpec(shape, lambda i: (0,) * len(shape))
    out_sds = jax.ShapeDtypeStruct((t, BRANCH_W), BF16)
    normalize = input_ln is not None
    ln_args = list(input_ln) if normalize else []
    return pl.pallas_call(
        functools.partial(_mixer_in_kernel, seq_len // TOKEN_TILE, normalize),
        grid=(t // TOKEN_TILE,),
        in_specs=[row] + [full((1, D_MODEL))] * len(ln_args) + [
            full((D_MODEL, MIX_COLS)),
            full((3, BRANCH_W)),
            full((N_GROUPS_C, CHUNK, CHUNK)),
            full((CHUNK, BRANCH_W)),
            full((1, BRANCH_W)),
            full((1, BRANCH_W)),
        ],
        out_specs=[row] * normalize + [row_out] * 5,
        out_shape=[jax.ShapeDtypeStruct((t, D_MODEL), F32)] * normalize + [out_sds] * 5,
        scratch_shapes=[pltpu.VMEM((SUBLANES, BRANCH_W), F32)],
        compiler_params=_params(("arbitrary",)),
        name="mixer_in",
    )(h, *ln_args, w_mix, conv_w, sg_w, sg_b_full, sg_ln_g, sg_ln_b)


def _attn_kernel(q_ref, k_ref, v_ref, o_ref, acc_ref, spent_ref):
    lane_head = lax.broadcasted_iota(jnp.int32, (1, BRANCH_W), 1) // HEAD_DIM
    t_idx = lax.broadcasted_iota(jnp.int32, (N_HEADS * Q_TILE, K_TILE), 0) % Q_TILE
    s_idx = lax.broadcasted_iota(jnp.int32, (N_HEADS * Q_TILE, K_TILE), 1)
    below_diagonal = s_idx < t_idx
    j_op = lax.broadcasted_iota(jnp.int32, (2 * K_TILE, 2 * K_TILE), 0) % K_TILE
    s_op = lax.broadcasted_iota(jnp.int32, (2 * K_TILE, 2 * K_TILE), 1)
    suffix_op = jnp.where((s_op >= K_TILE) | (j_op > s_op), 1.0, 0.0).astype(BF16)

    def visit(q_stack, kb, spent, diagonal):
        start = pl.multiple_of(jnp.maximum(kb, 0) * K_TILE, K_TILE)
        k_blk = k_ref[0, pl.ds(start, K_TILE), :]
        v_blk = v_ref[0, pl.ds(start, K_TILE), :]
        z = lax.dot_general(q_stack, k_blk, (((1,), (1,)), ((), ())), preferred_element_type=F32)
        softplus = jnp.maximum(z, 0.0) + jnp.log2(1.0 + jnp.exp2(-jnp.abs(z)))
        if diagonal:
            softplus = jnp.where(below_diagonal, softplus, 0.0)
        else:
            spent = spent + jnp.where(kb >= 0, 0.0, NO_TILE_BITS)
        hi = softplus.astype(BF16)
        lo = (softplus - hi.astype(F32)).astype(BF16)
        sums = jnp.dot(jnp.concatenate([hi, lo], axis=1), suffix_op, preferred_element_type=F32)
        a = jnp.exp2(z - softplus - (sums[:, :K_TILE] + spent))
        if diagonal:
            a = jnp.where(below_diagonal, a, 0.0)
        a = a.astype(BF16)
        a_heads = jnp.concatenate([a[h * Q_TILE:(h + 1) * Q_TILE] for h in range(N_HEADS)], axis=1)
        v_heads = jnp.concatenate(
            [jnp.where(lane_head == h, v_blk, jnp.zeros_like(v_blk)) for h in range(N_HEADS)], axis=0)
        out = jnp.dot(a_heads, v_heads, preferred_element_type=F32)
        return spent + sums[:, K_TILE:], out

    def stacked_queries(sub):
        q = q_ref[0, sub * Q_TILE:(sub + 1) * Q_TILE, :]
        return jnp.concatenate(
            [jnp.where(lane_head == h, q, jnp.zeros_like(q)) for h in range(N_HEADS)], axis=0)

    first_tile = pl.program_id(1) * Q_SUBTILES
    least = None
    for sub in range(Q_SUBTILES):
        tile = first_tile + sub
        q_stack = stacked_queries(sub)
        spent, out0 = visit(q_stack, tile, jnp.zeros((N_HEADS * Q_TILE, K_TILE), F32), True)
        spent, out1 = visit(q_stack, tile - 1, spent, False)
        acc_ref[sub] = out0 + out1
        spent_ref[sub] = spent
        sub_least = jnp.min(spent)
        least = sub_least if least is None else jnp.minimum(least, sub_least)

    def walk(state):
        step, _ = state
        least = None
        for sub in range(Q_SUBTILES):
            spent, out = visit(stacked_queries(sub), first_tile + sub - step, spent_ref[sub], False)
            acc_ref[sub] += out
            spent_ref[sub] = spent
            sub_least = jnp.min(spent)
            least = sub_least if least is None else jnp.minimum(least, sub_least)
        return step + 1, least

    def unfinished(state):
        _, least = state
        return least < EXIT_BITS

    lax.while_loop(unfinished, walk, (jnp.int32(2), least))
    for sub in range(Q_SUBTILES):
        o_ref[0, sub * Q_TILE:(sub + 1) * Q_TILE, :] = acc_ref[sub].astype(BF16)


def _attn_call(q, k, v):
    b, s, _ = q.shape
    rows = Q_SUBTILES * Q_TILE
    tile = pl.BlockSpec((1, rows, BRANCH_W), lambda bi, qi: (bi, qi, 0))
    seq = pl.BlockSpec((1, s, BRANCH_W), lambda bi, qi: (bi, 0, 0))
    return pl.pallas_call(
        _attn_kernel,
        grid=(b, s // rows),
        in_specs=[tile, seq, seq],
        out_specs=tile,
        out_shape=jax.ShapeDtypeStruct((b, s, BRANCH_W), BF16),
        scratch_shapes=[pltpu.VMEM((Q_SUBTILES, Q_TILE, BRANCH_W), F32),
                        pltpu.VMEM((Q_SUBTILES, N_HEADS * Q_TILE, K_TILE), F32)],
        compiler_params=_params(("parallel", "parallel")),
        name="attn",
    )(q, k, v)


def _merge_kernel(h_ref, ya_ref, yb_ref, yc_ref, wg_ref, bgate_ref, wbr_ref, wo_ref, g_ref, b_ref,
                  o_ref):
    for rows in _slabs(h_ref.shape[0]):
        h = h_ref[rows, :]
        logits = jnp.dot(h.astype(BF16), wg_ref[...], preferred_element_type=F32)
        merged = None
        for i, y_ref in enumerate((ya_ref, yb_ref, yc_ref)):
            gate = jax.nn.sigmoid(logits[:, i * D_MODEL:(i + 1) * D_MODEL] + bgate_ref[i:i + 1, :])
            term = gate * jnp.dot(y_ref[rows, :], wbr_ref[i], preferred_element_type=F32)
            merged = term if merged is None else merged + term
        mix = jnp.dot(merged.astype(BF16), wo_ref[...], preferred_element_type=F32)
        o_ref[rows, :] = _layer_norm(DEEPNORM_ALPHA * h + mix, g_ref[...], b_ref[...])


def _merge_call(h, y_a, y_b, y_c, w_gate_cols, b_gate, w_br, w_o, ln_g, ln_b):
    t = h.shape[0]
    row = pl.BlockSpec((CHANNEL_TILE, D_MODEL), lambda i: (i, 0))
    branch = pl.BlockSpec((CHANNEL_TILE, BRANCH_W), lambda i: (i, 0))
    full = lambda shape: pl.BlockSpec(shape, lambda i: (0,) * len(shape))
    return pl.pallas_call(
        _merge_kernel,
        grid=(t // CHANNEL_TILE,),
        in_specs=[row, branch, branch, branch,
                  full((D_MODEL, N_BRANCH * D_MODEL)),
                  full((N_BRANCH, D_MODEL)),
                  full((N_BRANCH, BRANCH_W, D_MODEL)),
                  full((D_MODEL, D_MODEL)),
                  full((1, D_MODEL)), full((1, D_MODEL))],
        out_specs=row,
        out_shape=jax.ShapeDtypeStruct((t, D_MODEL), F32),
        compiler_params=_params(("parallel",)),
        name="merge",
    )(h, y_a, y_b, y_c, w_gate_cols, b_gate, w_br, w_o, ln_g, ln_b)


def _first_index_of_max(x, lane):
    m = jnp.max(x, axis=1, keepdims=True)
    idx = jnp.min(jnp.where(x == m, lane, LANES), axis=1, keepdims=True)
    return m, idx


def _moe_kernel(x_ref, wr_ref, rb_ref, wg_ref, wu_ref, wd_ref, g_ref, b_ref, o_ref):
    for rows in _slabs(x_ref.shape[0]):
        _moe_slab(x_ref, wr_ref, rb_ref, wg_ref, wu_ref, wd_ref, g_ref, b_ref, o_ref, rows)


def _moe_slab(x_ref, wr_ref, rb_ref, wg_ref, wu_ref, wd_ref, g_ref, b_ref, o_ref, rows):
    x = x_ref[rows, :]
    xb = x.astype(BF16)
    x_lo = (x - xb.astype(F32)).astype(BF16)
    r = jnp.dot(xb, wr_ref[...], preferred_element_type=F32)
    logits = (r[:, :LANES] + r[:, LANES:]
              + jnp.dot(x_lo, wr_ref[:, :LANES], preferred_element_type=F32))
    scores = jax.nn.sigmoid(logits)
    sel = scores + rb_ref[...]
    lane = lax.broadcasted_iota(jnp.int32, sel.shape, 1)
    group_of_lane = lane // EXPERTS_PER_GROUP
    best = None
    for g in range(N_EXPERT_GROUPS):
        sel_g = jnp.where(group_of_lane == g, sel, -jnp.inf)
        m1, i1 = _first_index_of_max(sel_g, lane)
        m2, i2 = _first_index_of_max(jnp.where(lane == i1, -jnp.inf, sel_g), lane)
        cand = (m1 + m2, i1, i2)
        if best is None:
            best = cand
        else:
            better = cand[0] > best[0]
            best = tuple(jnp.where(better, c, o) for c, o in zip(cand, best))
    _, i1, i2 = best
    s1 = jnp.sum(jnp.where(lane == i1, scores, 0.0), axis=1, keepdims=True)
    s2 = jnp.sum(jnp.where(lane == i2, scores, 0.0), axis=1, keepdims=True)
    denom = s1 + s2
    dense_w = jnp.where(lane == i1, s1 / denom, 0.0) + jnp.where(lane == i2, s2 / denom, 0.0)

    group_cols = EXPERTS_PER_GROUP * D_FF_EXPERT
    y = None
    for g in range(N_EXPERT_GROUPS):
        cols = slice(g * group_cols, (g + 1) * group_cols)
        gate = jnp.dot(xb, wg_ref[:, cols], preferred_element_type=F32)
        up = jnp.dot(xb, wu_ref[:, cols], preferred_element_type=F32)
        w_cols = jnp.concatenate(
            [jnp.broadcast_to(dense_w[:, e:e + 1], (x.shape[0], D_FF_EXPERT))
             for e in range(g * EXPERTS_PER_GROUP, (g + 1) * EXPERTS_PER_GROUP)], axis=1)
        hidden = (jax.nn.silu(gate) * up * w_cols).astype(BF16)
        part = jnp.dot(hidden, wd_ref[cols, :], preferred_element_type=F32)
        y = part if y is None else y + part
    o_ref[rows, :] = _layer_norm(DEEPNORM_ALPHA * x + y, g_ref[...], b_ref[...])


def _moe_call(x, wr_split, rb_pad, w_gate2, w_up2, w_down2, ln_g, ln_b):
    t = x.shape[0]
    row = pl.BlockSpec((CHANNEL_TILE, D_MODEL), lambda i: (i, 0))
    full = lambda shape: pl.BlockSpec(shape, lambda i: (0,) * len(shape))
    return pl.pallas_call(
        _moe_kernel,
        grid=(t // CHANNEL_TILE,),
        in_specs=[row,
                  full((D_MODEL, 2 * LANES)), full((1, LANES)),
                  full((D_MODEL, FF_ALL)), full((D_MODEL, FF_ALL)), full((FF_ALL, D_MODEL)),
                  full((1, D_MODEL)), full((1, D_MODEL))],
        out_specs=row,
        out_shape=jax.ShapeDtypeStruct((t, D_MODEL), F32),
        compiler_params=_params(("parallel",)),
        name="moe",
    )(x, wr_split, rb_pad, w_gate2, w_up2, w_down2, ln_g, ln_b)


def kernel(x, ln_in_g, ln_in_b, w_in, b_gate, conv_w, sg_w, sg_b, sg_ln_g, sg_ln_b, w_br, w_o,
           ln_mix_g, ln_mix_b, w_router, router_bias, w_gate, w_up, w_down, ln_ffn_g, ln_ffn_b):
    batch, seq_len, d = x.shape
    assert d == D_MODEL and seq_len % TOKEN_TILE == 0 and seq_len % (Q_SUBTILES * Q_TILE) == 0
    t = batch * seq_len
    vec = lambda a: a.reshape(1, -1)

    wr_pad = jnp.pad(w_router, ((0, 0), (0, LANES - N_EXPERTS)))
    wr_hi = wr_pad.astype(BF16)
    wr_split = jnp.concatenate([wr_hi, (wr_pad - wr_hi.astype(F32)).astype(BF16)], axis=1)
    rb_pad = jnp.pad(router_bias, (0, LANES - N_EXPERTS)).reshape(1, LANES)

    h = x.reshape(t, d)
    for l in range(DEPTH):
        w_mix = w_in[l, :, :MIX_COLS].astype(BF16)
        w_gate_cols = w_in[l, :, MIX_COLS:].astype(BF16)
        sg_b_full = jnp.repeat(sg_b[l].T, GROUP_DIM_C, axis=1)
        mixer_args = (w_mix, conv_w[l], sg_w[l], sg_b_full, vec(sg_ln_g[l]), vec(sg_ln_b[l]), seq_len)
        if l == 0:
            h, y_a, y_c, q, k, v = _mixer_in_call(h, *mixer_args,
                                                  input_ln=(vec(ln_in_g), vec(ln_in_b)))
        else:
            y_a, y_c, q, k, v = _mixer_in_call(h, *mixer_args)
        shape3 = (batch, seq_len, BRANCH_W)
        y_b = _attn_call(q.reshape(shape3), k.reshape(shape3), v.reshape(shape3)).reshape(t, BRANCH_W)
        h = _merge_call(h, y_a, y_b, y_c, w_gate_cols, b_gate[l], w_br[l].astype(BF16),
                        w_o[l].astype(BF16), vec(ln_mix_g[l]), vec(ln_mix_b[l]))
        w_gate2 = w_gate[l].transpose(1, 0, 2).reshape(d, FF_ALL).astype(BF16)
        w_up2 = w_up[l].transpose(1, 0, 2).reshape(d, FF_ALL).astype(BF16)
        w_down2 = w_down[l].reshape(FF_ALL, d).astype(BF16)
        h = _moe_call(h, wr_split, rb_pad, w_gate2, w_up2, w_down2,
                      vec(ln_ffn_g[l]), vec(ln_ffn_b[l]))
    return h.reshape(batch, seq_len, d)
```

```python
import functools

import jax
import jax.numpy as jnp
from jax import lax
from jax.experimental import pallas as pl
from jax.experimental.pallas import tpu as pltpu

D_MODEL = 1024
DEPTH = 4
BRANCH_W = 256
N_HEADS = 4
HEAD_DIM = 64
CHUNK = 128
N_GROUPS_C = 4
GROUP_DIM_C = 64
MIX_COLS = 8 * BRANCH_W
N_BRANCH = 3
N_EXPERTS = 16
EXPERTS_PER_GROUP = 4
N_EXPERT_GROUPS = 4
D_FF_EXPERT = 128
FF_ALL = N_EXPERTS * D_FF_EXPERT
DEEPNORM_ALPHA = (2 * DEPTH) ** 0.25
LN_EPS = 1e-5

LANES = 128
SUBLANES = 8
VMEM_LIMIT = 56 * 1024 * 1024

TOKEN_TILE = 512
CHANNEL_TILE = 1024
SORT_ALIGN = 16
SORT_CHUNK = 9 * SORT_ALIGN
Q_TILE = 128
K_TILE = 128
Q_SUBTILES = 2
assert Q_TILE == K_TILE
EXIT_BITS = 127.0
NO_TILE_BITS = 16384.0
LOG2_E = 1.4426950408889634

F32 = jnp.float32
BF16 = jnp.bfloat16


def _layer_norm(x, g, b):
    mu = jnp.mean(x, axis=-1, keepdims=True)
    xc = x - mu
    var = jnp.mean(xc * xc, axis=-1, keepdims=True)
    return xc * lax.rsqrt(var + LN_EPS) * g + b


def _slabs(n_rows):
    return [slice(r, r + TOKEN_TILE) for r in range(0, n_rows, TOKEN_TILE)]


def _params(sem):
    return pltpu.CompilerParams(dimension_semantics=sem, vmem_limit_bytes=VMEM_LIMIT)


def _shift_rows(u, prev, shift):
    rolled = pltpu.roll(u, shift, axis=0)
    head_rows = lax.broadcasted_iota(jnp.int32, (SUBLANES, u.shape[1]), 0)
    head = jnp.where(head_rows < shift, pltpu.roll(prev, shift, axis=0), rolled[:SUBLANES])
    return jnp.concatenate([head, rolled[SUBLANES:]], axis=0)


def _mixer_in_kernel(tiles_per_seq, normalize_input, *refs):
    if normalize_input:
        (h_ref, ing_ref, inb_ref, w_ref, convw_ref, sgw_ref, sgb_ref, lng_ref, lnb_ref,
         h_out_ref, ya_ref, yc_ref, q_ref, k_ref, v_ref, tail_ref) = refs
    else:
        (h_ref, w_ref, convw_ref, sgw_ref, sgb_ref, lng_ref, lnb_ref,
         ya_ref, yc_ref, q_ref, k_ref, v_ref, tail_ref) = refs

    @pl.when(pl.program_id(0) % tiles_per_seq == 0)
    def _():
        tail_ref[...] = jnp.zeros_like(tail_ref)

    h = h_ref[...]
    if normalize_input:
        h = _layer_norm(h, ing_ref[...], inb_ref[...])
        h_out_ref[...] = h
    hb = h.astype(BF16)
    proj_c = jnp.dot(hb, w_ref[:, 6 * BRANCH_W:], preferred_element_type=F32)
    proj_a = jnp.dot(hb, w_ref[:, :3 * BRANCH_W], preferred_element_type=F32)
    proj_b = jnp.dot(hb, w_ref[:, 3 * BRANCH_W:6 * BRANCH_W], preferred_element_type=F32)
    col = lambda p, j: p[:, j * BRANCH_W:(j + 1) * BRANCH_W]

    gate_u = jax.nn.gelu(col(proj_c, 0))
    vn = _layer_norm(jax.nn.gelu(col(proj_c, 1)), lng_ref[...], lnb_ref[...]).astype(BF16)
    t_idx = lax.broadcasted_iota(jnp.int32, (CHUNK, CHUNK), 0)
    s_idx = lax.broadcasted_iota(jnp.int32, (CHUNK, CHUNK), 1)
    w_cat = jnp.concatenate(
        [jnp.where(s_idx <= t_idx, sgw_ref[g], 0.0).astype(BF16) for g in range(N_GROUPS_C)], axis=1)
    lane_group = lax.broadcasted_iota(jnp.int32, (1, BRANCH_W), 1) // GROUP_DIM_C
    for c in range(hb.shape[0] // CHUNK):
        rows = slice(c * CHUNK, (c + 1) * CHUNK)
        v_chunk = vn[rows]
        v_groups = jnp.concatenate(
            [jnp.where(lane_group == g, v_chunk, jnp.zeros_like(v_chunk)) for g in range(N_GROUPS_C)],
            axis=0)
        f = jnp.dot(w_cat, v_groups, preferred_element_type=F32) + sgb_ref[...]
        yc_ref[rows, :] = (gate_u[rows] * f).astype(BF16)

    u = col(proj_a, 2) * col(proj_a, 0)
    prev = tail_ref[...]
    conv = (convw_ref[0:1, :] * _shift_rows(u, prev, 2)
            + convw_ref[1:2, :] * _shift_rows(u, prev, 1)
            + convw_ref[2:3, :] * u)
    tail_ref[...] = u[u.shape[0] - SUBLANES:]
    ya_ref[...] = (col(proj_a, 1) * conv).astype(BF16)

    q_ref[...] = (col(proj_b, 0) * (HEAD_DIM ** -0.5 * LOG2_E)).astype(BF16)
    k_ref[...] = col(proj_b, 1).astype(BF16)
    v_ref[...] = col(proj_b, 2).astype(BF16)


def _mixer_in_call(h, w_mix, conv_w, sg_w, sg_b_full, sg_ln_g, sg_ln_b, seq_len, input_ln=None):
    t = h.shape[0]
    row = pl.BlockSpec((TOKEN_TILE, D_MODEL), lambda i: (i, 0))
    row_out = pl.BlockSpec((TOKEN_TILE, BRANCH_W), lambda i: (i, 0))
    full = lambda shape: pl.BlockSpec(shape, lambda i: (0,) * len(shape))
    out_sds = jax.ShapeDtypeStruct((t, BRANCH_W), BF16)
    normalize = input_ln is not None
    ln_args = list(input_ln) if normalize else []
    return pl.pallas_call(
        functools.partial(_mixer_in_kernel, seq_len // TOKEN_TILE, normalize),
        grid=(t // TOKEN_TILE,),
        in_specs=[row] + [full((1, D_MODEL))] * len(ln_args) + [
            full((D_MODEL, MIX_COLS)),
            full((3, BRANCH_W)),
            full((N_GROUPS_C, CHUNK, CHUNK)),
            full((CHUNK, BRANCH_W)),
            full((1, BRANCH_W)),
            full((1, BRANCH_W)),
        ],
        out_specs=[row] * normalize + [row_out] * 5,
        out_shape=[jax.ShapeDtypeStruct((t, D_MODEL), F32)] * normalize + [out_sds] * 5,
        scratch_shapes=[pltpu.VMEM((SUBLANES, BRANCH_W), F32)],
        compiler_params=_params(("arbitrary",)),
        name="mixer_in",
    )(h, *ln_args, w_mix, conv_w, sg_w, sg_b_full, sg_ln_g, sg_ln_b)


def _attn_kernel(q_ref, k_ref, v_ref, o_ref, acc_ref, spent_ref):
    lane_head = lax.broadcasted_iota(jnp.int32, (1, BRANCH_W), 1) // HEAD_DIM
    t_idx = lax.broadcasted_iota(jnp.int32, (N_HEADS * Q_TILE, K_TILE), 0) % Q_TILE
    s_idx = lax.broadcasted_iota(jnp.int32, (N_HEADS * Q_TILE, K_TILE), 1)
    below_diagonal = s_idx < t_idx
    j_op = lax.broadcasted_iota(jnp.int32, (2 * K_TILE, 2 * K_TILE), 0) % K_TILE
    s_op = lax.broadcasted_iota(jnp.int32, (2 * K_TILE, 2 * K_TILE), 1)
    suffix_op = jnp.where((s_op >= K_TILE) | (j_op > s_op), 1.0, 0.0).astype(BF16)

    def visit(q_stack, kb, spent, diagonal):
        start = pl.multiple_of(jnp.maximum(kb, 0) * K_TILE, K_TILE)
        k_blk = k_ref[0, pl.ds(start, K_TILE), :]
        v_blk = v_ref[0, pl.ds(start, K_TILE), :]
        z = lax.dot_general(q_stack, k_blk, (((1,), (1,)), ((), ())), preferred_element_type=F32)
        softplus = jnp.maximum(z, 0.0) + jnp.log2(1.0 + jnp.exp2(-jnp.abs(z)))
        if diagonal:
            softplus = jnp.where(below_diagonal, softplus, 0.0)
        else:
            spent = spent + jnp.where(kb >= 0, 0.0, NO_TILE_BITS)
        hi = softplus.astype(BF16)
        lo = (softplus - hi.astype(F32)).astype(BF16)
        sums = jnp.dot(jnp.concatenate([hi, lo], axis=1), suffix_op, preferred_element_type=F32)
        a = jnp.exp2(z - softplus - (sums[:, :K_TILE] + spent))
        if diagonal:
            a = jnp.where(below_diagonal, a, 0.0)
        a = a.astype(BF16)
        a_heads = jnp.concatenate([a[h * Q_TILE:(h + 1) * Q_TILE] for h in range(N_HEADS)], axis=1)
        v_heads = jnp.concatenate(
            [jnp.where(lane_head == h, v_blk, jnp.zeros_like(v_blk)) for h in range(N_HEADS)], axis=0)
        out = jnp.dot(a_heads, v_heads, preferred_element_type=F32)
        return spent + sums[:, K_TILE:], out

    def stacked_queries(sub):
        q = q_ref[0, sub * Q_TILE:(sub + 1) * Q_TILE, :]
        return jnp.concatenate(
            [jnp.where(lane_head == h, q, jnp.zeros_like(q)) for h in range(N_HEADS)], axis=0)

    first_tile = pl.program_id(1) * Q_SUBTILES
    least = None
    for sub in range(Q_SUBTILES):
        tile = first_tile + sub
        q_stack = stacked_queries(sub)
        spent, out0 = visit(q_stack, tile, jnp.zeros((N_HEADS * Q_TILE, K_TILE), F32), True)
        spent, out1 = visit(q_stack, tile - 1, spent, False)
        acc_ref[sub] = out0 + out1
        spent_ref[sub] = spent
        sub_least = jnp.min(spent)
        least = sub_least if least is None else jnp.minimum(least, sub_least)

    def walk(state):
        step, _ = state
        least = None
        for sub in range(Q_SUBTILES):
            spent, out = visit(stacked_queries(sub), first_tile + sub - step, spent_ref[sub], False)
            acc_ref[sub] += out
            spent_ref[sub] = spent
            sub_least = jnp.min(spent)
            least = sub_least if least is None else jnp.minimum(least, sub_least)
        return step + 1, least

    def unfinished(state):
        _, least = state
        return least < EXIT_BITS

    lax.while_loop(unfinished, walk, (jnp.int32(2), least))
    for sub in range(Q_SUBTILES):
        o_ref[0, sub * Q_TILE:(sub + 1) * Q_TILE, :] = acc_ref[sub].astype(BF16)


def _attn_call(q, k, v):
    b, s, _ = q.shape
    rows = Q_SUBTILES * Q_TILE
    tile = pl.BlockSpec((1, rows, BRANCH_W), lambda bi, qi: (bi, qi, 0))
    seq = pl.BlockSpec((1, s, BRANCH_W), lambda bi, qi: (bi, 0, 0))
    return pl.pallas_call(
        _attn_kernel,
        grid=(b, s // rows),
        in_specs=[tile, seq, seq],
        out_specs=tile,
        out_shape=jax.ShapeDtypeStruct((b, s, BRANCH_W), BF16),
        scratch_shapes=[pltpu.VMEM((Q_SUBTILES, Q_TILE, BRANCH_W), F32),
                        pltpu.VMEM((Q_SUBTILES, N_HEADS * Q_TILE, K_TILE), F32)],
        compiler_params=_params(("parallel", "parallel")),
        name="attn",
    )(q, k, v)


def _merge_kernel(h_ref, ya_ref, yb_ref, yc_ref, wg_ref, bgate_ref, wbr_ref, wo_ref, g_ref, b_ref,
                  o_ref):
    for rows in _slabs(h_ref.shape[0]):
        h = h_ref[rows, :]
        logits = jnp.dot(h.astype(BF16), wg_ref[...], preferred_element_type=F32)
        merged = None
        for i, y_ref in enumerate((ya_ref, yb_ref, yc_ref)):
            gate = jax.nn.sigmoid(logits[:, i * D_MODEL:(i + 1) * D_MODEL] + bgate_ref[i:i + 1, :])
            term = gate * jnp.dot(y_ref[rows, :], wbr_ref[i], preferred_element_type=F32)
            merged = term if merged is None else merged + term
        mix = jnp.dot(merged.astype(BF16), wo_ref[...], preferred_element_type=F32)
        o_ref[rows, :] = _layer_norm(DEEPNORM_ALPHA * h + mix, g_ref[...], b_ref[...])


def _merge_call(h, y_a, y_b, y_c, w_gate_cols, b_gate, w_br, w_o, ln_g, ln_b):
    t = h.shape[0]
    row = pl.BlockSpec((CHANNEL_TILE, D_MODEL), lambda i: (i, 0))
    branch = pl.BlockSpec((CHANNEL_TILE, BRANCH_W), lambda i: (i, 0))
    full = lambda shape: pl.BlockSpec(shape, lambda i: (0,) * len(shape))
    return pl.pallas_call(
        _merge_kernel,
        grid=(t // CHANNEL_TILE,),
        in_specs=[row, branch, branch, branch,
                  full((D_MODEL, N_BRANCH * D_MODEL)),
                  full((N_BRANCH, D_MODEL)),
                  full((N_BRANCH, BRANCH_W, D_MODEL)),
                  full((D_MODEL, D_MODEL)),
                  full((1, D_MODEL)), full((1, D_MODEL))],
        out_specs=row,
        out_shape=jax.ShapeDtypeStruct((t, D_MODEL), F32),
        compiler_params=_params(("parallel",)),
        name="merge",
    )(h, y_a, y_b, y_c, w_gate_cols, b_gate, w_br, w_o, ln_g, ln_b)


def _top2_of_group(sel, scores):
    def first_argmax(vals):
        top = functools.reduce(jnp.maximum, vals)
        pos = jnp.full(top.shape, len(vals) - 1, jnp.int32)
        for j in reversed(range(len(vals) - 1)):
            pos = jnp.where(vals[j] == top, j, pos)
        return top, pos

    def pick(rows, pos):
        out = rows[-1]
        for j in reversed(range(len(rows) - 1)):
            out = jnp.where(pos == j, rows[j], out)
        return out

    m1, i1 = first_argmax(sel)
    m2, i2 = first_argmax([jnp.where(i1 == j, -jnp.inf, v) for j, v in enumerate(sel)])
    return m1 + m2, i1, i2, pick(scores, i1), pick(scores, i2)


def _moe_kernel(x_ref, wr_ref, rb_ref, prefix_ref, wg_ref, wu_ref, wd_ref, g_ref, b_ref, o_ref,
                rows_ref, xs_ref, ws_ref, ys_ref):
    n_tok = x_ref.shape[0]
    x = x_ref[...]
    xb = x.astype(BF16)
    x_lo = (x - xb.astype(F32)).astype(BF16)
    r = jnp.dot(xb, wr_ref[...], preferred_element_type=F32)
    logits = (r[:, :LANES] + r[:, LANES:]
              + jnp.dot(x_lo, wr_ref[:, :LANES], preferred_element_type=F32))

    scores_t = jax.nn.sigmoid(logits.T[:N_EXPERTS])
    rows_ref[:N_EXPERTS, :] = scores_t
    rows_ref[N_EXPERTS:2 * N_EXPERTS, :] = scores_t + rb_ref[...]
    score_rows = [rows_ref[e:e + 1, :] for e in range(N_EXPERTS)]
    sel_rows = [rows_ref[N_EXPERTS + e:N_EXPERTS + e + 1, :] for e in range(N_EXPERTS)]
    best = None
    for g in range(N_EXPERT_GROUPS):
        members = slice(g * EXPERTS_PER_GROUP, (g + 1) * EXPERTS_PER_GROUP)
        cand = _top2_of_group(sel_rows[members], score_rows[members])
        cand = cand + (jnp.full(cand[1].shape, g, jnp.int32),)
        if best is None:
            best = cand
        else:
            better = cand[0] > best[0]
            best = tuple(jnp.where(better, c, o) for c, o in zip(cand, best))
    _, i1, i2, s1, s2, group = best
    denom = s1 + s2
    w1, w2 = s1 / denom, s2 / denom
    dense_rows = [
        jnp.where(group == e // EXPERTS_PER_GROUP,
                  jnp.where(i1 == e % EXPERTS_PER_GROUP, w1, 0.0)
                  + jnp.where(i2 == e % EXPERTS_PER_GROUP, w2, 0.0), 0.0)
        for e in range(N_EXPERTS)]

    in_group = [jnp.where(group == g, 1.0, 0.0) for g in range(N_EXPERT_GROUPS)]
    rows_ref[...] = jnp.zeros_like(rows_ref)
    for g in range(N_EXPERT_GROUPS):
        rows_ref[g:g + 1, :] = in_group[g]
    onehot_t = rows_ref[:2 * SUBLANES, :].astype(BF16)
    earlier = jnp.dot(onehot_t, prefix_ref[...], preferred_element_type=F32)
    counts = [jnp.sum(m).astype(jnp.int32) for m in in_group]
    starts, nxt = [], jnp.int32(0)
    for g in range(N_EXPERT_GROUPS):
        starts.append(nxt)
        nxt = nxt + counts[g]
    dest_t = sum(in_group[g] * (earlier[g:g + 1] + starts[g].astype(F32))
                 for g in range(N_EXPERT_GROUPS))

    for e in range(N_EXPERTS):
        rows_ref[e:e + 1, :] = dense_rows[e]
    rows_ref[N_EXPERTS:N_EXPERTS + 1, :] = dest_t
    cols = rows_ref[...].T
    dest_col = cols[:, N_EXPERTS:N_EXPERTS + 1].astype(jnp.int32)
    to_sorted = jnp.where(
        lax.broadcasted_iota(jnp.int32, (n_tok, n_tok), 0) == dest_t.astype(jnp.int32),
        1.0, 0.0).astype(BF16)
    from_sorted = jnp.where(
        lax.broadcasted_iota(jnp.int32, (n_tok, n_tok), 1) == dest_col, 1.0, 0.0).astype(BF16)

    xs_ref[:n_tok, :] = jnp.dot(to_sorted, xb, preferred_element_type=F32).astype(BF16)
    xs_ref[n_tok:, :] = jnp.zeros((SORT_CHUNK, D_MODEL), BF16)
    w_hi = cols.astype(BF16)
    w_split = jnp.concatenate([w_hi, (cols - w_hi.astype(F32)).astype(BF16)], axis=1)
    w_sorted = jnp.dot(to_sorted, w_split, preferred_element_type=F32)
    ws_ref[:n_tok, :] = w_sorted[:, :LANES] + w_sorted[:, LANES:]
    ws_ref[n_tok:, :] = jnp.zeros((SORT_CHUNK, LANES), F32)
    ys_ref[...] = jnp.zeros_like(ys_ref)

    group_cols = EXPERTS_PER_GROUP * D_FF_EXPERT

    def piece(g, c):
        cols_g = slice(g * group_cols, (g + 1) * group_cols)
        r0 = pl.multiple_of(starts[g] // SORT_ALIGN * SORT_ALIGN + c * SORT_CHUNK, SORT_ALIGN)
        xs = xs_ref[pl.ds(r0, SORT_CHUNK), :]
        gate = jnp.dot(xs, wg_ref[:, cols_g], preferred_element_type=F32)
        up = jnp.dot(xs, wu_ref[:, cols_g], preferred_element_type=F32)
        wc = ws_ref[pl.ds(r0, SORT_CHUNK), :]
        w_cols = jnp.concatenate(
            [jnp.broadcast_to(wc[:, e:e + 1], (SORT_CHUNK, D_FF_EXPERT))
             for e in range(g * EXPERTS_PER_GROUP, (g + 1) * EXPERTS_PER_GROUP)], axis=1)
        hidden = (jax.nn.silu(gate) * up * w_cols).astype(BF16)
        ys_ref[pl.ds(r0, SORT_CHUNK), :] += jnp.dot(hidden, wd_ref[cols_g, :],
                                                     preferred_element_type=F32)

    for g in range(N_EXPERT_GROUPS):
        piece(g, 0)
    for g in range(N_EXPERT_GROUPS):
        span = counts[g] + starts[g] % SORT_ALIGN
        lax.fori_loop(1, (span + (SORT_CHUNK - 1)) // SORT_CHUNK,
                      lambda c, carry, g=g: (piece(g, c), carry)[1], 0)

    y = jnp.dot(from_sorted, ys_ref[:n_tok, :].astype(BF16), preferred_element_type=F32)
    o_ref[...] = _layer_norm(DEEPNORM_ALPHA * x + y, g_ref[...], b_ref[...])


def _moe_call(x, wr_split, rb_col, prefix_op, w_gate2, w_up2, w_down2, ln_g, ln_b):
    t = x.shape[0]
    row = pl.BlockSpec((TOKEN_TILE, D_MODEL), lambda i: (i, 0))
    full = lambda shape: pl.BlockSpec(shape, lambda i: (0,) * len(shape))
    return pl.pallas_call(
        _moe_kernel,
        grid=(t // TOKEN_TILE,),
        in_specs=[row,
                  full((D_MODEL, 2 * LANES)), full((N_EXPERTS, 1)), full((TOKEN_TILE, TOKEN_TILE)),
                  full((D_MODEL, FF_ALL)), full((D_MODEL, FF_ALL)), full((FF_ALL, D_MODEL)),
                  full((1, D_MODEL)), full((1, D_MODEL))],
        out_specs=row,
        out_shape=jax.ShapeDtypeStruct((t, D_MODEL), F32),
        scratch_shapes=[pltpu.VMEM((LANES, TOKEN_TILE), F32),
                        pltpu.VMEM((TOKEN_TILE + SORT_CHUNK, D_MODEL), BF16),
                        pltpu.VMEM((TOKEN_TILE + SORT_CHUNK, LANES), F32),
                        pltpu.VMEM((TOKEN_TILE + SORT_CHUNK, D_MODEL), F32)],
        compiler_params=_params(("parallel",)),
        name="moe",
    )(x, wr_split, rb_col, prefix_op, w_gate2, w_up2, w_down2, ln_g, ln_b)


def kernel(x, ln_in_g, ln_in_b, w_in, b_gate, conv_w, sg_w, sg_b, sg_ln_g, sg_ln_b, w_br, w_o,
           ln_mix_g, ln_mix_b, w_router, router_bias, w_gate, w_up, w_down, ln_ffn_g, ln_ffn_b):
    batch, seq_len, d = x.shape
    assert d == D_MODEL and seq_len % TOKEN_TILE == 0 and seq_len % (Q_SUBTILES * Q_TILE) == 0
    t = batch * seq_len
    vec = lambda a: a.reshape(1, -1)

    wr_pad = jnp.pad(w_router, ((0, 0), (0, LANES - N_EXPERTS)))
    wr_hi = wr_pad.astype(BF16)
    wr_split = jnp.concatenate([wr_hi, (wr_pad - wr_hi.astype(F32)).astype(BF16)], axis=1)
    rb_col = router_bias.reshape(N_EXPERTS, 1)
    prefix_op = (lax.broadcasted_iota(jnp.int32, (TOKEN_TILE, TOKEN_TILE), 0)
                 < lax.broadcasted_iota(jnp.int32, (TOKEN_TILE, TOKEN_TILE), 1)).astype(BF16)

    h = x.reshape(t, d)
    for l in range(DEPTH):
        w_mix = w_in[l, :, :MIX_COLS].astype(BF16)
        w_gate_cols = w_in[l, :, MIX_COLS:].astype(BF16)
        sg_b_full = jnp.repeat(sg_b[l].T, GROUP_DIM_C, axis=1)
        mixer_args = (w_mix, conv_w[l], sg_w[l], sg_b_full, vec(sg_ln_g[l]), vec(sg_ln_b[l]), seq_len)
        if l == 0:
            h, y_a, y_c, q, k, v = _mixer_in_call(h, *mixer_args,
                                                  input_ln=(vec(ln_in_g), vec(ln_in_b)))
        else:
            y_a, y_c, q, k, v = _mixer_in_call(h, *mixer_args)
        shape3 = (batch, seq_len, BRANCH_W)
        y_b = _attn_call(q.reshape(shape3), k.reshape(shape3), v.reshape(shape3)).reshape(t, BRANCH_W)
        h = _merge_call(h, y_a, y_b, y_c, w_gate_cols, b_gate[l], w_br[l].astype(BF16),
                        w_o[l].astype(BF16), vec(ln_mix_g[l]), vec(ln_mix_b[l]))
        w_gate2 = w_gate[l].transpose(1, 0, 2).reshape(d, FF_ALL).astype(BF16)
        w_up2 = w_up[l].transpose(1, 0, 2).reshape(d, FF_ALL).astype(BF16)
        w_down2 = w_down[l].reshape(FF_ALL, d).astype(BF16)
        h = _moe_call(h, wr_split, rb_col, prefix_op, w_gate2, w_up2, w_down2,
                      vec(ln_ffn_g[l]), vec(ln_ffn_b[l]))
    return h.reshape(batch, seq_len, d)
```

```python
import functools

import jax
import jax.numpy as jnp
from jax import lax
from jax.experimental import pallas as pl
from jax.experimental.pallas import tpu as pltpu

D_MODEL = 1024
DEPTH = 4
BRANCH_W = 256
N_HEADS = 4
HEAD_DIM = 64
CHUNK = 128
N_GROUPS_C = 4
GROUP_DIM_C = 64
MIX_COLS = 8 * BRANCH_W
N_BRANCH = 3
N_EXPERTS = 16
EXPERTS_PER_GROUP = 4
N_EXPERT_GROUPS = 4
D_FF_EXPERT = 128
FF_ALL = N_EXPERTS * D_FF_EXPERT
DEEPNORM_ALPHA = (2 * DEPTH) ** 0.25
LN_EPS = 1e-5

LANES = 128
SUBLANES = 8
VMEM_LIMIT = 56 * 1024 * 1024

TOKEN_TILE = 512
CHANNEL_TILE = 1024
SORT_ALIGN = 16
SORT_CHUNK = 9 * SORT_ALIGN
Q_TILE = 128
K_TILE = 128
Q_SUBTILES = 4
assert Q_TILE == K_TILE
EXIT_BITS = 127.0
NO_TILE_BITS = 16384.0
LOG2_E = 1.4426950408889634

F32 = jnp.float32
BF16 = jnp.bfloat16


def _layer_norm(x, g, b):
    mu = jnp.mean(x, axis=-1, keepdims=True)
    xc = x - mu
    var = jnp.mean(xc * xc, axis=-1, keepdims=True)
    return xc * lax.rsqrt(var + LN_EPS) * g + b


def _slabs(n_rows):
    return [slice(r, r + TOKEN_TILE) for r in range(0, n_rows, TOKEN_TILE)]


def _params(sem):
    return pltpu.CompilerParams(dimension_semantics=sem, vmem_limit_bytes=VMEM_LIMIT)


def _shift_rows(u, prev, shift):
    rolled = pltpu.roll(u, shift, axis=0)
    head_rows = lax.broadcasted_iota(jnp.int32, (SUBLANES, u.shape[1]), 0)
    head = jnp.where(head_rows < shift, pltpu.roll(prev, shift, axis=0), rolled[:SUBLANES])
    return jnp.concatenate([head, rolled[SUBLANES:]], axis=0)


def _mixer_in_kernel(tiles_per_seq, normalize_input, *refs):
    if normalize_input:
        (h_ref, ing_ref, inb_ref, w32_ref, convw_ref, sgw_ref, sgb_ref, lng_ref, lnb_ref,
         h_out_ref, ya_ref, yc_ref, q_ref, k_ref, v_ref, tail_ref, w_ref) = refs
    else:
        (h_ref, w32_ref, convw_ref, sgw_ref, sgb_ref, lng_ref, lnb_ref,
         ya_ref, yc_ref, q_ref, k_ref, v_ref, tail_ref, w_ref) = refs

    @pl.when(pl.program_id(0) == 0)
    def _():
        w_ref[...] = w32_ref[0].astype(BF16)

    @pl.when(pl.program_id(0) % tiles_per_seq == 0)
    def _():
        tail_ref[...] = jnp.zeros_like(tail_ref)

    h = h_ref[...]
    if normalize_input:
        h = _layer_norm(h, ing_ref[...], inb_ref[...])
        h_out_ref[...] = h
    hb = h.astype(BF16)
    proj_c = jnp.dot(hb, w_ref[:, 6 * BRANCH_W:], preferred_element_type=F32)
    proj_a = jnp.dot(hb, w_ref[:, :3 * BRANCH_W], preferred_element_type=F32)
    proj_b = jnp.dot(hb, w_ref[:, 3 * BRANCH_W:6 * BRANCH_W], preferred_element_type=F32)
    col = lambda p, j: p[:, j * BRANCH_W:(j + 1) * BRANCH_W]

    gate_u = jax.nn.gelu(col(proj_c, 0))
    vn = _layer_norm(jax.nn.gelu(col(proj_c, 1)), lng_ref[...], lnb_ref[...]).astype(BF16)
    t_idx = lax.broadcasted_iota(jnp.int32, (CHUNK, CHUNK), 0)
    s_idx = lax.broadcasted_iota(jnp.int32, (CHUNK, CHUNK), 1)
    w_cat = jnp.concatenate(
        [jnp.where(s_idx <= t_idx, sgw_ref[g], 0.0).astype(BF16) for g in range(N_GROUPS_C)], axis=1)
    lane_group = lax.broadcasted_iota(jnp.int32, (1, BRANCH_W), 1) // GROUP_DIM_C
    for c in range(hb.shape[0] // CHUNK):
        rows = slice(c * CHUNK, (c + 1) * CHUNK)
        v_chunk = vn[rows]
        v_groups = jnp.concatenate(
            [jnp.where(lane_group == g, v_chunk, jnp.zeros_like(v_chunk)) for g in range(N_GROUPS_C)],
            axis=0)
        f = jnp.dot(w_cat, v_groups, preferred_element_type=F32) + sgb_ref[...]
        yc_ref[rows, :] = (gate_u[rows] * f).astype(BF16)

    u = col(proj_a, 2) * col(proj_a, 0)
    prev = tail_ref[...]
    conv = (convw_ref[0:1, :] * _shift_rows(u, prev, 2)
            + convw_ref[1:2, :] * _shift_rows(u, prev, 1)
            + convw_ref[2:3, :] * u)
    tail_ref[...] = u[u.shape[0] - SUBLANES:]
    ya_ref[...] = (col(proj_a, 1) * conv).astype(BF16)

    q_ref[...] = (col(proj_b, 0) * (HEAD_DIM ** -0.5 * LOG2_E)).astype(BF16)
    k_ref[...] = col(proj_b, 1).astype(BF16)
    v_ref[...] = col(proj_b, 2).astype(BF16)


def _mixer_in_call(h, w_in, layer, conv_w, sg_w, sg_b_full, sg_ln_g, sg_ln_b, seq_len,
                   input_ln=None):
    t = h.shape[0]
    row = pl.BlockSpec((TOKEN_TILE, D_MODEL), lambda i: (i, 0))
    row_out = pl.BlockSpec((TOKEN_TILE, BRANCH_W), lambda i: (i, 0))
    full = lambda shape: pl.BlockSpec(shape, lambda i: (0,) * len(shape))
    out_sds = jax.ShapeDtypeStruct((t, BRANCH_W), BF16)
    normalize = input_ln is not None
    ln_args = list(input_ln) if normalize else []
    return pl.pallas_call(
        functools.partial(_mixer_in_kernel, seq_len // TOKEN_TILE, normalize),
        grid=(t // TOKEN_TILE,),
        in_specs=[row] + [full((1, D_MODEL))] * len(ln_args) + [
            pl.BlockSpec((1, D_MODEL, MIX_COLS), lambda i: (layer, 0, 0),
                         pipeline_mode=pl.Buffered(1)),
            full((3, BRANCH_W)),
            full((N_GROUPS_C, CHUNK, CHUNK)),
            full((CHUNK, BRANCH_W)),
            full((1, BRANCH_W)),
            full((1, BRANCH_W)),
        ],
        out_specs=[row] * normalize + [row_out] * 5,
        out_shape=[jax.ShapeDtypeStruct((t, D_MODEL), F32)] * normalize + [out_sds] * 5,
        scratch_shapes=[pltpu.VMEM((SUBLANES, BRANCH_W), F32),
                        pltpu.VMEM((D_MODEL, MIX_COLS), BF16)],
        compiler_params=_params(("arbitrary",)),
        name="mixer_in",
    )(h, *ln_args, w_in, conv_w, sg_w, sg_b_full, sg_ln_g, sg_ln_b)


def _attn_kernel(q_ref, k_ref, v_ref, o_ref, acc_ref, spent_ref):
    lane_head = lax.broadcasted_iota(jnp.int32, (1, BRANCH_W), 1) // HEAD_DIM
    t_idx = lax.broadcasted_iota(jnp.int32, (N_HEADS * Q_TILE, K_TILE), 0) % Q_TILE
    s_idx = lax.broadcasted_iota(jnp.int32, (N_HEADS * Q_TILE, K_TILE), 1)
    below_diagonal = s_idx < t_idx
    j_op = lax.broadcasted_iota(jnp.int32, (2 * K_TILE, 2 * K_TILE), 0) % K_TILE
    s_op = lax.broadcasted_iota(jnp.int32, (2 * K_TILE, 2 * K_TILE), 1)
    suffix_op = jnp.where((s_op >= K_TILE) | (j_op > s_op), 1.0, 0.0).astype(BF16)

    def visit(q_stack, kb, spent, diagonal):
        start = pl.multiple_of(jnp.maximum(kb, 0) * K_TILE, K_TILE)
        k_blk = k_ref[0, pl.ds(start, K_TILE), :]
        v_blk = v_ref[0, pl.ds(start, K_TILE), :]
        z = lax.dot_general(q_stack, k_blk, (((1,), (1,)), ((), ())), preferred_element_type=F32)
        softplus = jnp.maximum(z, 0.0) + jnp.log2(1.0 + jnp.exp2(-jnp.abs(z)))
        if diagonal:
            softplus = jnp.where(below_diagonal, softplus, 0.0)
        else:
            spent = spent + jnp.where(kb >= 0, 0.0, NO_TILE_BITS)
        hi = softplus.astype(BF16)
        lo = (softplus - hi.astype(F32)).astype(BF16)
        sums = jnp.dot(jnp.concatenate([hi, lo], axis=1), suffix_op, preferred_element_type=F32)
        a = jnp.exp2(z - softplus - (sums[:, :K_TILE] + spent))
        if diagonal:
            a = jnp.where(below_diagonal, a, 0.0)
        a = a.astype(BF16)
        a_heads = jnp.concatenate([a[h * Q_TILE:(h + 1) * Q_TILE] for h in range(N_HEADS)], axis=1)
        v_heads = jnp.concatenate(
            [jnp.where(lane_head == h, v_blk, jnp.zeros_like(v_blk)) for h in range(N_HEADS)], axis=0)
        out = jnp.dot(a_heads, v_heads, preferred_element_type=F32)
        return spent + sums[:, K_TILE:], out

    def stacked_queries(sub):
        q = q_ref[0, sub * Q_TILE:(sub + 1) * Q_TILE, :]
        return jnp.concatenate(
            [jnp.where(lane_head == h, q, jnp.zeros_like(q)) for h in range(N_HEADS)], axis=0)

    first_tile = pl.program_id(1) * Q_SUBTILES
    least = []
    for sub in range(Q_SUBTILES):
        tile = first_tile + sub
        q_stack = stacked_queries(sub)
        spent, out0 = visit(q_stack, tile, jnp.zeros((N_HEADS * Q_TILE, K_TILE), F32), True)
        spent, out1 = visit(q_stack, tile - 1, spent, False)
        acc_ref[sub] = out0 + out1
        spent_ref[sub] = spent
        least.append(jnp.min(spent))

    def walk(state):
        step, least = state

        def advance(sub):
            spent, out = visit(stacked_queries(sub), first_tile + sub - step, spent_ref[sub], False)
            acc_ref[sub] += out
            spent_ref[sub] = spent
            return jnp.min(spent)

        return step + 1, tuple(
            lax.cond(least[sub] < EXIT_BITS, functools.partial(advance, sub), lambda sub=sub: least[sub])
            for sub in range(Q_SUBTILES))

    def unfinished(state):
        _, least = state
        return functools.reduce(jnp.minimum, least) < EXIT_BITS

    lax.while_loop(unfinished, walk, (jnp.int32(2), tuple(least)))
    for sub in range(Q_SUBTILES):
        o_ref[0, sub * Q_TILE:(sub + 1) * Q_TILE, :] = acc_ref[sub].astype(BF16)


def _attn_call(q, k, v):
    b, s, _ = q.shape
    rows = Q_SUBTILES * Q_TILE
    tile = pl.BlockSpec((1, rows, BRANCH_W), lambda bi, qi: (bi, qi, 0))
    seq = pl.BlockSpec((1, s, BRANCH_W), lambda bi, qi: (bi, 0, 0))
    return pl.pallas_call(
        _attn_kernel,
        grid=(b, s // rows),
        in_specs=[tile, seq, seq],
        out_specs=tile,
        out_shape=jax.ShapeDtypeStruct((b, s, BRANCH_W), BF16),
        scratch_shapes=[pltpu.VMEM((Q_SUBTILES, Q_TILE, BRANCH_W), F32),
                        pltpu.VMEM((Q_SUBTILES, N_HEADS * Q_TILE, K_TILE), F32)],
        compiler_params=_params(("parallel", "parallel")),
        name="attn",
    )(q, k, v)


def _merge_kernel(h_ref, ya_ref, yb_ref, yc_ref, wga32_ref, wgb32_ref, wgc32_ref, bgate_ref,
                  wbr32_ref, wo32_ref, g_ref, b_ref, o_ref, wg_ref, wbr_ref, wo_ref):
    @pl.when(pl.program_id(0) == 0)
    def _():
        for i, w32_ref in enumerate((wga32_ref, wgb32_ref, wgc32_ref)):
            wg_ref[:, i * D_MODEL:(i + 1) * D_MODEL] = w32_ref[0].astype(BF16)
        wbr_ref[...] = wbr32_ref[0].astype(BF16)
        wo_ref[...] = wo32_ref[0].astype(BF16)

    for rows in _slabs(h_ref.shape[0]):
        h = h_ref[rows, :]
        logits = jnp.dot(h.astype(BF16), wg_ref[...], preferred_element_type=F32)
        merged = None
        for i, y_ref in enumerate((ya_ref, yb_ref, yc_ref)):
            gate = jax.nn.sigmoid(logits[:, i * D_MODEL:(i + 1) * D_MODEL] + bgate_ref[i:i + 1, :])
            term = gate * jnp.dot(y_ref[rows, :], wbr_ref[i], preferred_element_type=F32)
            merged = term if merged is None else merged + term
        mix = jnp.dot(merged.astype(BF16), wo_ref[...], preferred_element_type=F32)
        o_ref[rows, :] = _layer_norm(DEEPNORM_ALPHA * h + mix, g_ref[...], b_ref[...])


def _merge_call(h, y_a, y_b, y_c, w_in, w_br, w_o, layer, b_gate, ln_g, ln_b):
    t = h.shape[0]
    row = pl.BlockSpec((CHANNEL_TILE, D_MODEL), lambda i: (i, 0))
    branch = pl.BlockSpec((CHANNEL_TILE, BRANCH_W), lambda i: (i, 0))
    full = lambda shape: pl.BlockSpec(shape, lambda i: (0,) * len(shape))
    resident = lambda shape, index: pl.BlockSpec(shape, index, pipeline_mode=pl.Buffered(1))
    gate_cols = [resident((1, D_MODEL, D_MODEL), lambda i, j=j: (layer, 0, MIX_COLS // D_MODEL + j))
                 for j in range(N_BRANCH)]
    return pl.pallas_call(
        _merge_kernel,
        grid=(t // CHANNEL_TILE,),
        in_specs=[row, branch, branch, branch, *gate_cols,
                  full((N_BRANCH, D_MODEL)),
                  resident((1, N_BRANCH, BRANCH_W, D_MODEL), lambda i: (layer, 0, 0, 0)),
                  resident((1, D_MODEL, D_MODEL), lambda i: (layer, 0, 0)),
                  full((1, D_MODEL)), full((1, D_MODEL))],
        out_specs=row,
        out_shape=jax.ShapeDtypeStruct((t, D_MODEL), F32),
        scratch_shapes=[pltpu.VMEM((D_MODEL, N_BRANCH * D_MODEL), BF16),
                        pltpu.VMEM((N_BRANCH, BRANCH_W, D_MODEL), BF16),
                        pltpu.VMEM((D_MODEL, D_MODEL), BF16)],
        compiler_params=_params(("arbitrary",)),
        name="merge",
    )(h, y_a, y_b, y_c, w_in, w_in, w_in, b_gate, w_br, w_o, ln_g, ln_b)


def _top2_of_group(sel, scores):
    def first_argmax(vals):
        top = functools.reduce(jnp.maximum, vals)
        pos = jnp.full(top.shape, len(vals) - 1, jnp.int32)
        for j in reversed(range(len(vals) - 1)):
            pos = jnp.where(vals[j] == top, j, pos)
        return top, pos

    def pick(rows, pos):
        out = rows[-1]
        for j in reversed(range(len(rows) - 1)):
            out = jnp.where(pos == j, rows[j], out)
        return out

    m1, i1 = first_argmax(sel)
    m2, i2 = first_argmax([jnp.where(i1 == j, -jnp.inf, v) for j, v in enumerate(sel)])
    return m1 + m2, i1, i2, pick(scores, i1), pick(scores, i2)


def _moe_kernel(x_ref, wr_ref, rb_ref, prefix_ref, wg_ref, wu_ref, wd_ref, g_ref, b_ref, o_ref,
                rows_ref, xs_ref, ws_ref, ys_ref, unsort_ref):
    slabs = range(x_ref.shape[0] // TOKEN_TILE)
    group_cols = EXPERTS_PER_GROUP * D_FF_EXPERT

    def piece(slab, starts, g, c):
        cols_g = slice(g * group_cols, (g + 1) * group_cols)
        r0 = pl.multiple_of(starts[g] // SORT_ALIGN * SORT_ALIGN + c * SORT_CHUNK, SORT_ALIGN)
        xs = xs_ref[slab, pl.ds(r0, SORT_CHUNK), :]
        gate = jnp.dot(xs, wg_ref[:, cols_g], preferred_element_type=F32)
        up = jnp.dot(xs, wu_ref[:, cols_g], preferred_element_type=F32)
        wc = ws_ref[slab, pl.ds(r0, SORT_CHUNK), :]
        w_cols = jnp.concatenate(
            [jnp.broadcast_to(wc[:, e:e + 1], (SORT_CHUNK, D_FF_EXPERT))
             for e in range(g * EXPERTS_PER_GROUP, (g + 1) * EXPERTS_PER_GROUP)], axis=1)
        hidden = (jax.nn.silu(gate) * up * w_cols).astype(BF16)
        ys_ref[slab, pl.ds(r0, SORT_CHUNK), :] += jnp.dot(hidden, wd_ref[cols_g, :],
                                                           preferred_element_type=F32)

    layout = []
    for slab in slabs:
        counts, starts = _moe_route_and_sort(slab, x_ref, wr_ref, rb_ref, prefix_ref,
                                             rows_ref, xs_ref, ws_ref, ys_ref, unsort_ref)
        for g in range(N_EXPERT_GROUPS):
            piece(slab, starts, g, 0)
        layout.append((counts, starts))

    for slab in slabs:
        counts, starts = layout[slab]
        for g in range(N_EXPERT_GROUPS):
            span = counts[g] + starts[g] % SORT_ALIGN
            lax.fori_loop(1, (span + (SORT_CHUNK - 1)) // SORT_CHUNK,
                          lambda c, carry, slab=slab, starts=starts, g=g:
                          (piece(slab, starts, g, c), carry)[1], 0)

    for slab in slabs:
        rows = slice(slab * TOKEN_TILE, (slab + 1) * TOKEN_TILE)
        y = jnp.dot(unsort_ref[slab], ys_ref[slab, :TOKEN_TILE, :].astype(BF16),
                    preferred_element_type=F32)
        o_ref[rows, :] = _layer_norm(DEEPNORM_ALPHA * x_ref[rows, :] + y, g_ref[...], b_ref[...])


def _moe_route_and_sort(slab, x_ref, wr_ref, rb_ref, prefix_ref,
                        rows_ref, xs_ref, ws_ref, ys_ref, unsort_ref):
    n_tok = TOKEN_TILE
    rows_ref = rows_ref.at[slab]
    x = x_ref[slab * TOKEN_TILE:(slab + 1) * TOKEN_TILE, :]
    xb = x.astype(BF16)
    x_lo = (x - xb.astype(F32)).astype(BF16)
    r = jnp.dot(xb, wr_ref[...], preferred_element_type=F32)
    logits = (r[:, :LANES] + r[:, LANES:]
              + jnp.dot(x_lo, wr_ref[:, :LANES], preferred_element_type=F32))

    scores_t = jax.nn.sigmoid(logits.T[:N_EXPERTS])
    rows_ref[:N_EXPERTS, :] = scores_t
    rows_ref[N_EXPERTS:2 * N_EXPERTS, :] = scores_t + rb_ref[...]
    score_rows = [rows_ref[e:e + 1, :] for e in range(N_EXPERTS)]
    sel_rows = [rows_ref[N_EXPERTS + e:N_EXPERTS + e + 1, :] for e in range(N_EXPERTS)]
    best = None
    for g in range(N_EXPERT_GROUPS):
        members = slice(g * EXPERTS_PER_GROUP, (g + 1) * EXPERTS_PER_GROUP)
        cand = _top2_of_group(sel_rows[members], score_rows[members])
        cand = cand + (jnp.full(cand[1].shape, g, jnp.int32),)
        if best is None:
            best = cand
        else:
            better = cand[0] > best[0]
            best = tuple(jnp.where(better, c, o) for c, o in zip(cand, best))
    _, i1, i2, s1, s2, group = best
    denom = s1 + s2
    w1, w2 = s1 / denom, s2 / denom
    dense_rows = [
        jnp.where(group == e // EXPERTS_PER_GROUP,
                  jnp.where(i1 == e % EXPERTS_PER_GROUP, w1, 0.0)
                  + jnp.where(i2 == e % EXPERTS_PER_GROUP, w2, 0.0), 0.0)
        for e in range(N_EXPERTS)]

    in_group = [jnp.where(group == g, 1.0, 0.0) for g in range(N_EXPERT_GROUPS)]
    rows_ref[...] = jnp.zeros_like(rows_ref)
    for g in range(N_EXPERT_GROUPS):
        rows_ref[g:g + 1, :] = in_group[g]
    onehot_t = rows_ref[:2 * SUBLANES, :].astype(BF16)
    earlier = jnp.dot(onehot_t, prefix_ref[...], preferred_element_type=F32)
    counts = [jnp.sum(m).astype(jnp.int32) for m in in_group]
    starts, nxt = [], jnp.int32(0)
    for g in range(N_EXPERT_GROUPS):
        starts.append(nxt)
        nxt = nxt + counts[g]
    dest_t = sum(in_group[g] * (earlier[g:g + 1] + starts[g].astype(F32))
                 for g in range(N_EXPERT_GROUPS))

    for e in range(N_EXPERTS):
        rows_ref[e:e + 1, :] = dense_rows[e]
    rows_ref[N_EXPERTS:N_EXPERTS + 1, :] = dest_t
    cols = rows_ref[...].T
    dest_col = cols[:, N_EXPERTS:N_EXPERTS + 1].astype(jnp.int32)
    to_sorted = jnp.where(
        lax.broadcasted_iota(jnp.int32, (n_tok, n_tok), 0) == dest_t.astype(jnp.int32),
        1.0, 0.0).astype(BF16)
    unsort_ref[slab] = jnp.where(
        lax.broadcasted_iota(jnp.int32, (n_tok, n_tok), 1) == dest_col, 1.0, 0.0).astype(BF16)

    xs_ref[slab, :n_tok, :] = jnp.dot(to_sorted, xb, preferred_element_type=F32).astype(BF16)
    xs_ref[slab, n_tok:, :] = jnp.zeros((SORT_CHUNK, D_MODEL), BF16)
    w_hi = cols.astype(BF16)
    w_split = jnp.concatenate([w_hi, (cols - w_hi.astype(F32)).astype(BF16)], axis=1)
    w_sorted = jnp.dot(to_sorted, w_split, preferred_element_type=F32)
    ws_ref[slab, :n_tok, :] = w_sorted[:, :LANES] + w_sorted[:, LANES:]
    ws_ref[slab, n_tok:, :] = jnp.zeros((SORT_CHUNK, LANES), F32)
    ys_ref[slab] = jnp.zeros(ys_ref.shape[1:], F32)
    return counts, starts


def _moe_call(x, wr_split, rb_col, prefix_op, w_gate2, w_up2, w_down2, ln_g, ln_b):
    t = x.shape[0]
    row = pl.BlockSpec((CHANNEL_TILE, D_MODEL), lambda i: (i, 0))
    full = lambda shape: pl.BlockSpec(shape, lambda i: (0,) * len(shape))
    n_slabs = CHANNEL_TILE // TOKEN_TILE
    return pl.pallas_call(
        _moe_kernel,
        grid=(t // CHANNEL_TILE,),
        in_specs=[row,
                  full((D_MODEL, 2 * LANES)), full((N_EXPERTS, 1)), full((TOKEN_TILE, TOKEN_TILE)),
                  full((D_MODEL, FF_ALL)), full((D_MODEL, FF_ALL)), full((FF_ALL, D_MODEL)),
                  full((1, D_MODEL)), full((1, D_MODEL))],
        out_specs=row,
        out_shape=jax.ShapeDtypeStruct((t, D_MODEL), F32),
        scratch_shapes=[pltpu.VMEM((n_slabs, LANES, TOKEN_TILE), F32),
                        pltpu.VMEM((n_slabs, TOKEN_TILE + SORT_CHUNK, D_MODEL), BF16),
                        pltpu.VMEM((n_slabs, TOKEN_TILE + SORT_CHUNK, LANES), F32),
                        pltpu.VMEM((n_slabs, TOKEN_TILE + SORT_CHUNK, D_MODEL), F32),
                        pltpu.VMEM((n_slabs, TOKEN_TILE, TOKEN_TILE), BF16)],
        compiler_params=_params(("parallel",)),
        name="moe",
    )(x, wr_split, rb_col, prefix_op, w_gate2, w_up2, w_down2, ln_g, ln_b)


def kernel(x, ln_in_g, ln_in_b, w_in, b_gate, conv_w, sg_w, sg_b, sg_ln_g, sg_ln_b, w_br, w_o,
           ln_mix_g, ln_mix_b, w_router, router_bias, w_gate, w_up, w_down, ln_ffn_g, ln_ffn_b):
    batch, seq_len, d = x.shape
    assert d == D_MODEL and seq_len % TOKEN_TILE == 0 and seq_len % (Q_SUBTILES * Q_TILE) == 0
    t = batch * seq_len
    vec = lambda a: a.reshape(1, -1)

    wr_pad = jnp.pad(w_router, ((0, 0), (0, LANES - N_EXPERTS)))
    wr_hi = wr_pad.astype(BF16)
    wr_split = jnp.concatenate([wr_hi, (wr_pad - wr_hi.astype(F32)).astype(BF16)], axis=1)
    rb_col = router_bias.reshape(N_EXPERTS, 1)
    prefix_op = (lax.broadcasted_iota(jnp.int32, (TOKEN_TILE, TOKEN_TILE), 0)
                 < lax.broadcasted_iota(jnp.int32, (TOKEN_TILE, TOKEN_TILE), 1)).astype(BF16)

    h = x.reshape(t, d)
    for l in range(DEPTH):
        sg_b_full = jnp.repeat(sg_b[l].T, GROUP_DIM_C, axis=1)
        mixer_args = (w_in, l, conv_w[l], sg_w[l], sg_b_full, vec(sg_ln_g[l]), vec(sg_ln_b[l]),
                      seq_len)
        if l == 0:
            h, y_a, y_c, q, k, v = _mixer_in_call(h, *mixer_args,
                                                  input_ln=(vec(ln_in_g), vec(ln_in_b)))
        else:
            y_a, y_c, q, k, v = _mixer_in_call(h, *mixer_args)
        shape3 = (batch, seq_len, BRANCH_W)
        y_b = _attn_call(q.reshape(shape3), k.reshape(shape3), v.reshape(shape3)).reshape(t, BRANCH_W)
        h = _merge_call(h, y_a, y_b, y_c, w_in, w_br, w_o, l, b_gate[l],
                        vec(ln_mix_g[l]), vec(ln_mix_b[l]))
        w_gate2 = w_gate[l].transpose(1, 0, 2).reshape(d, FF_ALL).astype(BF16)
        w_up2 = w_up[l].transpose(1, 0, 2).reshape(d, FF_ALL).astype(BF16)
        w_down2 = w_down[l].reshape(FF_ALL, d).astype(BF16)
        h = _moe_call(h, wr_split, rb_col, prefix_op, w_gate2, w_up2, w_down2,
                      vec(ln_ffn_g[l]), vec(ln_ffn_b[l]))
    return h.reshape(batch, seq_len, d)
```

```python
import functools

import jax
import jax.numpy as jnp
from jax import lax
from jax.experimental import pallas as pl
from jax.experimental.pallas import tpu as pltpu

D_MODEL = 1024
DEPTH = 4
BRANCH_W = 256
N_HEADS = 4
HEAD_DIM = 64
CHUNK = 128
N_GROUPS_C = 4
GROUP_DIM_C = 64
MIX_COLS = 8 * BRANCH_W
N_BRANCH = 3
N_EXPERTS = 16
EXPERTS_PER_GROUP = 4
N_EXPERT_GROUPS = 4
D_FF_EXPERT = 128
FF_ALL = N_EXPERTS * D_FF_EXPERT
DEEPNORM_ALPHA = (2 * DEPTH) ** 0.25
LN_EPS = 1e-5

LANES = 128
SUBLANES = 8
VMEM_LIMIT = 56 * 1024 * 1024

TOKEN_TILE = 512
CHANNEL_TILE = 1024
SORT_ALIGN = 16
SORT_CHUNK = 10 * SORT_ALIGN
Q_TILE = 128
K_TILE = 128
Q_SUBTILES = 4
assert Q_TILE == K_TILE
EXIT_BITS = 127.0
NO_TILE_BITS = 16384.0
LOG2_E = 1.4426950408889634

F32 = jnp.float32
BF16 = jnp.bfloat16


def _layer_norm(x, g, b):
    mu = jnp.mean(x, axis=-1, keepdims=True)
    xc = x - mu
    var = jnp.mean(xc * xc, axis=-1, keepdims=True)
    return xc * lax.rsqrt(var + LN_EPS) * g + b


def _slabs(n_rows):
    return [slice(r, r + TOKEN_TILE) for r in range(0, n_rows, TOKEN_TILE)]


def _params(sem):
    return pltpu.CompilerParams(dimension_semantics=sem, vmem_limit_bytes=VMEM_LIMIT)


def _shift_rows(u, prev, shift):
    rolled = pltpu.roll(u, shift, axis=0)
    head_rows = lax.broadcasted_iota(jnp.int32, (SUBLANES, u.shape[1]), 0)
    head = jnp.where(head_rows < shift, pltpu.roll(prev, shift, axis=0), rolled[:SUBLANES])
    return jnp.concatenate([head, rolled[SUBLANES:]], axis=0)


def _mixer_in_kernel(tiles_per_seq, normalize_input, *refs):
    if normalize_input:
        (h_ref, ing_ref, inb_ref, w32_ref, convw_ref, sgw_ref, sgb_ref, lng_ref, lnb_ref,
         h_out_ref, ya_ref, yc_ref, q_ref, k_ref, v_ref, tail_ref, w_ref) = refs
    else:
        (h_ref, w32_ref, convw_ref, sgw_ref, sgb_ref, lng_ref, lnb_ref,
         ya_ref, yc_ref, q_ref, k_ref, v_ref, tail_ref, w_ref) = refs

    @pl.when(pl.program_id(0) == 0)
    def _():
        w_ref[...] = w32_ref[0].astype(BF16)

    @pl.when(pl.program_id(0) % tiles_per_seq == 0)
    def _():
        tail_ref[...] = jnp.zeros_like(tail_ref)

    h = h_ref[...]
    if normalize_input:
        h = _layer_norm(h, ing_ref[...], inb_ref[...])
        h_out_ref[...] = h
    hb = h.astype(BF16)
    proj_c = jnp.dot(hb, w_ref[:, 6 * BRANCH_W:], preferred_element_type=F32)
    proj_a = jnp.dot(hb, w_ref[:, :3 * BRANCH_W], preferred_element_type=F32)
    proj_b = jnp.dot(hb, w_ref[:, 3 * BRANCH_W:6 * BRANCH_W], preferred_element_type=F32)
    col = lambda p, j: p[:, j * BRANCH_W:(j + 1) * BRANCH_W]

    gate_u = jax.nn.gelu(col(proj_c, 0))
    vn = _layer_norm(jax.nn.gelu(col(proj_c, 1)), lng_ref[...], lnb_ref[...]).astype(BF16)
    t_idx = lax.broadcasted_iota(jnp.int32, (CHUNK, CHUNK), 0)
    s_idx = lax.broadcasted_iota(jnp.int32, (CHUNK, CHUNK), 1)
    w_cat = jnp.concatenate(
        [jnp.where(s_idx <= t_idx, sgw_ref[g], 0.0).astype(BF16) for g in range(N_GROUPS_C)], axis=1)
    lane_group = lax.broadcasted_iota(jnp.int32, (1, BRANCH_W), 1) // GROUP_DIM_C
    for c in range(hb.shape[0] // CHUNK):
        rows = slice(c * CHUNK, (c + 1) * CHUNK)
        v_chunk = vn[rows]
        v_groups = jnp.concatenate(
            [jnp.where(lane_group == g, v_chunk, jnp.zeros_like(v_chunk)) for g in range(N_GROUPS_C)],
            axis=0)
        f = jnp.dot(w_cat, v_groups, preferred_element_type=F32) + sgb_ref[...]
        yc_ref[rows, :] = (gate_u[rows] * f).astype(BF16)

    u = col(proj_a, 2) * col(proj_a, 0)
    prev = tail_ref[...]
    conv = (convw_ref[0:1, :] * _shift_rows(u, prev, 2)
            + convw_ref[1:2, :] * _shift_rows(u, prev, 1)
            + convw_ref[2:3, :] * u)
    tail_ref[...] = u[u.shape[0] - SUBLANES:]
    ya_ref[...] = (col(proj_a, 1) * conv).astype(BF16)

    q_ref[...] = (col(proj_b, 0) * (HEAD_DIM ** -0.5 * LOG2_E)).astype(BF16)
    k_ref[...] = col(proj_b, 1).astype(BF16)
    v_ref[...] = col(proj_b, 2).astype(BF16)


def _mixer_in_call(h, w_in, layer, conv_w, sg_w, sg_b_full, sg_ln_g, sg_ln_b, seq_len,
                   input_ln=None):
    t = h.shape[0]
    row = pl.BlockSpec((TOKEN_TILE, D_MODEL), lambda i: (i, 0))
    row_out = pl.BlockSpec((TOKEN_TILE, BRANCH_W), lambda i: (i, 0))
    full = lambda shape: pl.BlockSpec(shape, lambda i: (0,) * len(shape))
    out_sds = jax.ShapeDtypeStruct((t, BRANCH_W), BF16)
    normalize = input_ln is not None
    ln_args = list(input_ln) if normalize else []
    return pl.pallas_call(
        functools.partial(_mixer_in_kernel, seq_len // TOKEN_TILE, normalize),
        grid=(t // TOKEN_TILE,),
        in_specs=[row] + [full((1, D_MODEL))] * len(ln_args) + [
            pl.BlockSpec((1, D_MODEL, MIX_COLS), lambda i: (layer, 0, 0),
                         pipeline_mode=pl.Buffered(1)),
            full((3, BRANCH_W)),
            full((N_GROUPS_C, CHUNK, CHUNK)),
            full((CHUNK, BRANCH_W)),
            full((1, BRANCH_W)),
            full((1, BRANCH_W)),
        ],
        out_specs=[row] * normalize + [row_out] * 5,
        out_shape=[jax.ShapeDtypeStruct((t, D_MODEL), F32)] * normalize + [out_sds] * 5,
        scratch_shapes=[pltpu.VMEM((SUBLANES, BRANCH_W), F32),
                        pltpu.VMEM((D_MODEL, MIX_COLS), BF16)],
        compiler_params=_params(("arbitrary",)),
        name="mixer_in",
    )(h, *ln_args, w_in, conv_w, sg_w, sg_b_full, sg_ln_g, sg_ln_b)


def _attn_kernel(q_ref, k_ref, v_ref, o_ref, acc_ref, spent_ref):
    lane_head = lax.broadcasted_iota(jnp.int32, (1, BRANCH_W), 1) // HEAD_DIM
    t_idx = lax.broadcasted_iota(jnp.int32, (N_HEADS * Q_TILE, K_TILE), 0) % Q_TILE
    s_idx = lax.broadcasted_iota(jnp.int32, (N_HEADS * Q_TILE, K_TILE), 1)
    below_diagonal = s_idx < t_idx
    j_op = lax.broadcasted_iota(jnp.int32, (2 * K_TILE, 2 * K_TILE), 0) % K_TILE
    s_op = lax.broadcasted_iota(jnp.int32, (2 * K_TILE, 2 * K_TILE), 1)
    suffix_op = jnp.where((s_op >= K_TILE) | (j_op > s_op), 1.0, 0.0).astype(BF16)

    def visit(q_stack, kb, spent, diagonal):
        start = pl.multiple_of(jnp.maximum(kb, 0) * K_TILE, K_TILE)
        k_blk = k_ref[0, pl.ds(start, K_TILE), :]
        v_blk = v_ref[0, pl.ds(start, K_TILE), :]
        z = lax.dot_general(q_stack, k_blk, (((1,), (1,)), ((), ())), preferred_element_type=F32)
        softplus = jnp.maximum(z, 0.0) + jnp.log2(1.0 + jnp.exp2(-jnp.abs(z)))
        if diagonal:
            softplus = jnp.where(below_diagonal, softplus, 0.0)
        else:
            spent = spent + jnp.where(kb >= 0, 0.0, NO_TILE_BITS)
        hi = softplus.astype(BF16)
        lo = (softplus - hi.astype(F32)).astype(BF16)
        sums = jnp.dot(jnp.concatenate([hi, lo], axis=1), suffix_op, preferred_element_type=F32)
        a = jnp.exp2(z - softplus - (sums[:, :K_TILE] + spent))
        if diagonal:
            a = jnp.where(below_diagonal, a, 0.0)
        a = a.astype(BF16)
        a_heads = jnp.concatenate([a[h * Q_TILE:(h + 1) * Q_TILE] for h in range(N_HEADS)], axis=1)
        v_heads = jnp.concatenate(
            [jnp.where(lane_head == h, v_blk, jnp.zeros_like(v_blk)) for h in range(N_HEADS)], axis=0)
        out = jnp.dot(a_heads, v_heads, preferred_element_type=F32)
        return spent + sums[:, K_TILE:], out

    def stacked_queries(sub):
        q = q_ref[0, sub * Q_TILE:(sub + 1) * Q_TILE, :]
        return jnp.concatenate(
            [jnp.where(lane_head == h, q, jnp.zeros_like(q)) for h in range(N_HEADS)], axis=0)

    first_tile = pl.program_id(1) * Q_SUBTILES
    least = []
    for sub in range(Q_SUBTILES):
        tile = first_tile + sub
        q_stack = stacked_queries(sub)
        spent, out0 = visit(q_stack, tile, jnp.zeros((N_HEADS * Q_TILE, K_TILE), F32), True)
        spent, out1 = visit(q_stack, tile - 1, spent, False)
        acc_ref[sub] = out0 + out1
        spent_ref[sub] = spent
        least.append(jnp.min(spent))

    def walk(state):
        step, least = state

        def advance(sub):
            spent, out = visit(stacked_queries(sub), first_tile + sub - step, spent_ref[sub], False)
            acc_ref[sub] += out
            spent_ref[sub] = spent
            return jnp.min(spent)

        return step + 1, tuple(
            lax.cond(least[sub] < EXIT_BITS, functools.partial(advance, sub), lambda sub=sub: least[sub])
            for sub in range(Q_SUBTILES))

    def unfinished(state):
        _, least = state
        return functools.reduce(jnp.minimum, least) < EXIT_BITS

    lax.while_loop(unfinished, walk, (jnp.int32(2), tuple(least)))
    for sub in range(Q_SUBTILES):
        o_ref[0, sub * Q_TILE:(sub + 1) * Q_TILE, :] = acc_ref[sub].astype(BF16)


def _attn_call(q, k, v):
    b, s, _ = q.shape
    rows = Q_SUBTILES * Q_TILE
    tile = pl.BlockSpec((1, rows, BRANCH_W), lambda bi, qi: (bi, qi, 0))
    seq = pl.BlockSpec((1, s, BRANCH_W), lambda bi, qi: (bi, 0, 0))
    return pl.pallas_call(
        _attn_kernel,
        grid=(b, s // rows),
        in_specs=[tile, seq, seq],
        out_specs=tile,
        out_shape=jax.ShapeDtypeStruct((b, s, BRANCH_W), BF16),
        scratch_shapes=[pltpu.VMEM((Q_SUBTILES, Q_TILE, BRANCH_W), F32),
                        pltpu.VMEM((Q_SUBTILES, N_HEADS * Q_TILE, K_TILE), F32)],
        compiler_params=_params(("parallel", "parallel")),
        name="attn",
    )(q, k, v)


def _merge_kernel(h_ref, ya_ref, yb_ref, yc_ref, wga32_ref, wgb32_ref, wgc32_ref, bgate_ref,
                  wbr32_ref, wo32_ref, g_ref, b_ref, o_ref, wg_ref, wbr_ref, wo_ref):
    @pl.when(pl.program_id(0) == 0)
    def _():
        for i, w32_ref in enumerate((wga32_ref, wgb32_ref, wgc32_ref)):
            wg_ref[:, i * D_MODEL:(i + 1) * D_MODEL] = w32_ref[0].astype(BF16)
        wbr_ref[...] = wbr32_ref[0].astype(BF16)
        wo_ref[...] = wo32_ref[0].astype(BF16)

    for rows in _slabs(h_ref.shape[0]):
        h = h_ref[rows, :]
        logits = jnp.dot(h.astype(BF16), wg_ref[...], preferred_element_type=F32)
        merged = None
        for i, y_ref in enumerate((ya_ref, yb_ref, yc_ref)):
            gate = 0.5 + 0.5 * jnp.tanh(
                0.5 * (logits[:, i * D_MODEL:(i + 1) * D_MODEL] + bgate_ref[i:i + 1, :]))
            term = gate * jnp.dot(y_ref[rows, :], wbr_ref[i], preferred_element_type=F32)
            merged = term if merged is None else merged + term
        mix = jnp.dot(merged.astype(BF16), wo_ref[...], preferred_element_type=F32)
        o_ref[rows, :] = _layer_norm(DEEPNORM_ALPHA * h + mix, g_ref[...], b_ref[...])


def _merge_call(h, y_a, y_b, y_c, w_in, w_br, w_o, layer, b_gate, ln_g, ln_b):
    t = h.shape[0]
    row = pl.BlockSpec((CHANNEL_TILE, D_MODEL), lambda i: (i, 0))
    branch = pl.BlockSpec((CHANNEL_TILE, BRANCH_W), lambda i: (i, 0))
    full = lambda shape: pl.BlockSpec(shape, lambda i: (0,) * len(shape))
    resident = lambda shape, index: pl.BlockSpec(shape, index, pipeline_mode=pl.Buffered(1))
    gate_cols = [resident((1, D_MODEL, D_MODEL), lambda i, j=j: (layer, 0, MIX_COLS // D_MODEL + j))
                 for j in range(N_BRANCH)]
    return pl.pallas_call(
        _merge_kernel,
        grid=(t // CHANNEL_TILE,),
        in_specs=[row, branch, branch, branch, *gate_cols,
                  full((N_BRANCH, D_MODEL)),
                  resident((1, N_BRANCH, BRANCH_W, D_MODEL), lambda i: (layer, 0, 0, 0)),
                  resident((1, D_MODEL, D_MODEL), lambda i: (layer, 0, 0)),
                  full((1, D_MODEL)), full((1, D_MODEL))],
        out_specs=row,
        out_shape=jax.ShapeDtypeStruct((t, D_MODEL), F32),
        scratch_shapes=[pltpu.VMEM((D_MODEL, N_BRANCH * D_MODEL), BF16),
                        pltpu.VMEM((N_BRANCH, BRANCH_W, D_MODEL), BF16),
                        pltpu.VMEM((D_MODEL, D_MODEL), BF16)],
        compiler_params=_params(("arbitrary",)),
        name="merge",
    )(h, y_a, y_b, y_c, w_in, w_in, w_in, b_gate, w_br, w_o, ln_g, ln_b)


def _top2_of_group(sel, scores):
    def first_argmax(vals):
        top = functools.reduce(jnp.maximum, vals)
        pos = jnp.full(top.shape, len(vals) - 1, jnp.int32)
        for j in reversed(range(len(vals) - 1)):
            pos = jnp.where(vals[j] == top, j, pos)
        return top, pos

    def pick(rows, pos):
        out = rows[-1]
        for j in reversed(range(len(rows) - 1)):
            out = jnp.where(pos == j, rows[j], out)
        return out

    m1, i1 = first_argmax(sel)
    m2, i2 = first_argmax([jnp.where(i1 == j, -jnp.inf, v) for j, v in enumerate(sel)])
    return m1 + m2, i1, i2, pick(scores, i1), pick(scores, i2)


def _moe_kernel(x_ref, wr_ref, rb_ref, prefix_ref, wg_ref, wu_ref, wd_ref, g_ref, b_ref, o_ref,
                rows_ref, xs_ref, ws_ref, ys_ref, unsort_ref):
    slabs = range(x_ref.shape[0] // TOKEN_TILE)
    group_cols = EXPERTS_PER_GROUP * D_FF_EXPERT

    def piece(slab, starts, g, c):
        cols_g = slice(g * group_cols, (g + 1) * group_cols)
        r0 = pl.multiple_of(starts[g] // SORT_ALIGN * SORT_ALIGN + c * SORT_CHUNK, SORT_ALIGN)
        xs = xs_ref[slab, pl.ds(r0, SORT_CHUNK), :]
        gate = jnp.dot(xs, wg_ref[:, cols_g], preferred_element_type=F32)
        up = jnp.dot(xs, wu_ref[:, cols_g], preferred_element_type=F32)
        wc = ws_ref[slab, pl.ds(r0, SORT_CHUNK), :]
        w_cols = jnp.concatenate(
            [jnp.broadcast_to(wc[:, e:e + 1], (SORT_CHUNK, D_FF_EXPERT))
             for e in range(g * EXPERTS_PER_GROUP, (g + 1) * EXPERTS_PER_GROUP)], axis=1)
        hidden = (jax.nn.silu(gate) * up * w_cols).astype(BF16)
        ys_ref[slab, pl.ds(r0, SORT_CHUNK), :] += jnp.dot(hidden, wd_ref[cols_g, :],
                                                           preferred_element_type=F32)

    layout = []
    for slab in slabs:
        counts, starts = _moe_route_and_sort(slab, x_ref, wr_ref, rb_ref, prefix_ref,
                                             rows_ref, xs_ref, ws_ref, ys_ref, unsort_ref)
        for g in range(N_EXPERT_GROUPS):
            piece(slab, starts, g, 0)
        layout.append((counts, starts))

    for slab in slabs:
        counts, starts = layout[slab]
        for g in range(N_EXPERT_GROUPS):
            span = counts[g] + starts[g] % SORT_ALIGN
            lax.fori_loop(1, (span + (SORT_CHUNK - 1)) // SORT_CHUNK,
                          lambda c, carry, slab=slab, starts=starts, g=g:
                          (piece(slab, starts, g, c), carry)[1], 0)

    for slab in slabs:
        rows = slice(slab * TOKEN_TILE, (slab + 1) * TOKEN_TILE)
        y = jnp.dot(unsort_ref[slab], ys_ref[slab, :TOKEN_TILE, :].astype(BF16),
                    preferred_element_type=F32)
        o_ref[rows, :] = _layer_norm(DEEPNORM_ALPHA * x_ref[rows, :] + y, g_ref[...], b_ref[...])


def _moe_route_and_sort(slab, x_ref, wr_ref, rb_ref, prefix_ref,
                        rows_ref, xs_ref, ws_ref, ys_ref, unsort_ref):
    n_tok = TOKEN_TILE
    rows_ref = rows_ref.at[slab]
    x = x_ref[slab * TOKEN_TILE:(slab + 1) * TOKEN_TILE, :]
    xb = x.astype(BF16)
    x_lo = (x - xb.astype(F32)).astype(BF16)
    r = jnp.dot(xb, wr_ref[...], preferred_element_type=F32)
    logits = (r[:, :LANES] + r[:, LANES:]
              + jnp.dot(x_lo, wr_ref[:, :LANES], preferred_element_type=F32))

    scores_t = jax.nn.sigmoid(logits.T[:N_EXPERTS])
    rows_ref[:N_EXPERTS, :] = scores_t
    rows_ref[N_EXPERTS:2 * N_EXPERTS, :] = scores_t + rb_ref[...]
    score_rows = [rows_ref[e:e + 1, :] for e in range(N_EXPERTS)]
    sel_rows = [rows_ref[N_EXPERTS + e:N_EXPERTS + e + 1, :] for e in range(N_EXPERTS)]
    best = None
    for g in range(N_EXPERT_GROUPS):
        members = slice(g * EXPERTS_PER_GROUP, (g + 1) * EXPERTS_PER_GROUP)
        cand = _top2_of_group(sel_rows[members], score_rows[members])
        cand = cand + (jnp.full(cand[1].shape, g, jnp.int32),)
        if best is None:
            best = cand
        else:
            better = cand[0] > best[0]
            best = tuple(jnp.where(better, c, o) for c, o in zip(cand, best))
    _, i1, i2, s1, s2, group = best
    denom = s1 + s2
    w1, w2 = s1 / denom, s2 / denom
    dense_rows = [
        jnp.where(group == e // EXPERTS_PER_GROUP,
                  jnp.where(i1 == e % EXPERTS_PER_GROUP, w1, 0.0)
                  + jnp.where(i2 == e % EXPERTS_PER_GROUP, w2, 0.0), 0.0)
        for e in range(N_EXPERTS)]

    in_group = [jnp.where(group == g, 1.0, 0.0) for g in range(N_EXPERT_GROUPS)]
    rows_ref[...] = jnp.zeros_like(rows_ref)
    for g in range(N_EXPERT_GROUPS):
        rows_ref[g:g + 1, :] = in_group[g]
    onehot_t = rows_ref[:2 * SUBLANES, :].astype(BF16)
    earlier = jnp.dot(onehot_t, prefix_ref[...], preferred_element_type=F32)
    counts = [jnp.sum(m).astype(jnp.int32) for m in in_group]
    starts, nxt = [], jnp.int32(0)
    for g in range(N_EXPERT_GROUPS):
        starts.append(nxt)
        nxt = nxt + counts[g]
    dest_t = sum(in_group[g] * (earlier[g:g + 1] + starts[g].astype(F32))
                 for g in range(N_EXPERT_GROUPS))

    for e in range(N_EXPERTS):
        rows_ref[e:e + 1, :] = dense_rows[e]
    rows_ref[N_EXPERTS:N_EXPERTS + 1, :] = dest_t
    cols = rows_ref[...].T
    dest_col = cols[:, N_EXPERTS:N_EXPERTS + 1].astype(jnp.int32)
    to_sorted = jnp.where(
        lax.broadcasted_iota(jnp.int32, (n_tok, n_tok), 0) == dest_t.astype(jnp.int32),
        1.0, 0.0).astype(BF16)
    unsort_ref[slab] = jnp.where(
        lax.broadcasted_iota(jnp.int32, (n_tok, n_tok), 1) == dest_col, 1.0, 0.0).astype(BF16)

    xs_ref[slab, :n_tok, :] = jnp.dot(to_sorted, xb, preferred_element_type=F32).astype(BF16)
    xs_ref[slab, n_tok:, :] = jnp.zeros((SORT_CHUNK, D_MODEL), BF16)
    w_hi = cols.astype(BF16)
    w_split = jnp.concatenate([w_hi, (cols - w_hi.astype(F32)).astype(BF16)], axis=1)
    w_sorted = jnp.dot(to_sorted, w_split, preferred_element_type=F32)
    ws_ref[slab, :n_tok, :] = w_sorted[:, :LANES] + w_sorted[:, LANES:]
    ws_ref[slab, n_tok:, :] = jnp.zeros((SORT_CHUNK, LANES), F32)
    ys_ref[slab] = jnp.zeros(ys_ref.shape[1:], F32)
    return counts, starts


def _moe_call(x, wr_split, rb_col, prefix_op, w_gate2, w_up2, w_down2, ln_g, ln_b):
    t = x.shape[0]
    row = pl.BlockSpec((CHANNEL_TILE, D_MODEL), lambda i: (i, 0))
    full = lambda shape: pl.BlockSpec(shape, lambda i: (0,) * len(shape))
    n_slabs = CHANNEL_TILE // TOKEN_TILE
    return pl.pallas_call(
        _moe_kernel,
        grid=(t // CHANNEL_TILE,),
        in_specs=[row,
                  full((D_MODEL, 2 * LANES)), full((N_EXPERTS, 1)), full((TOKEN_TILE, TOKEN_TILE)),
                  full((D_MODEL, FF_ALL)), full((D_MODEL, FF_ALL)), full((FF_ALL, D_MODEL)),
                  full((1, D_MODEL)), full((1, D_MODEL))],
        out_specs=row,
        out_shape=jax.ShapeDtypeStruct((t, D_MODEL), F32),
        scratch_shapes=[pltpu.VMEM((n_slabs, LANES, TOKEN_TILE), F32),
                        pltpu.VMEM((n_slabs, TOKEN_TILE + SORT_CHUNK, D_MODEL), BF16),
                        pltpu.VMEM((n_slabs, TOKEN_TILE + SORT_CHUNK, LANES), F32),
                        pltpu.VMEM((n_slabs, TOKEN_TILE + SORT_CHUNK, D_MODEL), F32),
                        pltpu.VMEM((n_slabs, TOKEN_TILE, TOKEN_TILE), BF16)],
        compiler_params=_params(("parallel",)),
        name="moe",
    )(x, wr_split, rb_col, prefix_op, w_gate2, w_up2, w_down2, ln_g, ln_b)


def kernel(x, ln_in_g, ln_in_b, w_in, b_gate, conv_w, sg_w, sg_b, sg_ln_g, sg_ln_b, w_br, w_o,
           ln_mix_g, ln_mix_b, w_router, router_bias, w_gate, w_up, w_down, ln_ffn_g, ln_ffn_b):
    batch, seq_len, d = x.shape
    assert d == D_MODEL and seq_len % TOKEN_TILE == 0 and seq_len % (Q_SUBTILES * Q_TILE) == 0
    t = batch * seq_len
    vec = lambda a: a.reshape(1, -1)

    wr_pad = jnp.pad(w_router, ((0, 0), (0, LANES - N_EXPERTS)))
    wr_hi = wr_pad.astype(BF16)
    wr_split = jnp.concatenate([wr_hi, (wr_pad - wr_hi.astype(F32)).astype(BF16)], axis=1)
    rb_col = router_bias.reshape(N_EXPERTS, 1)
    prefix_op = (lax.broadcasted_iota(jnp.int32, (TOKEN_TILE, TOKEN_TILE), 0)
                 < lax.broadcasted_iota(jnp.int32, (TOKEN_TILE, TOKEN_TILE), 1)).astype(BF16)

    h = x.reshape(t, d)
    for l in range(DEPTH):
        sg_b_full = jnp.repeat(sg_b[l].T, GROUP_DIM_C, axis=1)
        mixer_args = (w_in, l, conv_w[l], sg_w[l], sg_b_full, vec(sg_ln_g[l]), vec(sg_ln_b[l]),
                      seq_len)
        if l == 0:
            h, y_a, y_c, q, k, v = _mixer_in_call(h, *mixer_args,
                                                  input_ln=(vec(ln_in_g), vec(ln_in_b)))
        else:
            y_a, y_c, q, k, v = _mixer_in_call(h, *mixer_args)
        shape3 = (batch, seq_len, BRANCH_W)
        y_b = _attn_call(q.reshape(shape3), k.reshape(shape3), v.reshape(shape3)).reshape(t, BRANCH_W)
        h = _merge_call(h, y_a, y_b, y_c, w_in, w_br, w_o, l, b_gate[l],
                        vec(ln_mix_g[l]), vec(ln_mix_b[l]))
        w_gate2 = w_gate[l].transpose(1, 0, 2).reshape(d, FF_ALL).astype(BF16)
        w_up2 = w_up[l].transpose(1, 0, 2).reshape(d, FF_ALL).astype(BF16)
        w_down2 = w_down[l].reshape(FF_ALL, d).astype(BF16)
        h = _moe_call(h, wr_split, rb_col, prefix_op, w_gate2, w_up2, w_down2,
                      vec(ln_ffn_g[l]), vec(ln_ffn_b[l]))
    return h.reshape(batch, seq_len, d)
```

```python
import functools

import jax
import jax.numpy as jnp
from jax import lax
from jax.experimental import pallas as pl
from jax.experimental.pallas import tpu as pltpu

D_MODEL = 1024
DEPTH = 4
BRANCH_W = 256
N_HEADS = 4
HEAD_DIM = 64
CHUNK = 128
N_GROUPS_C = 4
GROUP_DIM_C = 64
MIX_COLS = 8 * BRANCH_W
N_BRANCH = 3
N_EXPERTS = 16
EXPERTS_PER_GROUP = 4
N_EXPERT_GROUPS = 4
D_FF_EXPERT = 128
FF_ALL = N_EXPERTS * D_FF_EXPERT
DEEPNORM_ALPHA = (2 * DEPTH) ** 0.25
LN_EPS = 1e-5

LANES = 128
SUBLANES = 8
VMEM_LIMIT = 56 * 1024 * 1024

TOKEN_TILE = 512
CHANNEL_TILE = 1024
SORT_ALIGN = 16
SORT_CHUNK = 10 * SORT_ALIGN
Q_TILE = 128
K_TILE = 128
Q_SUBTILES = 4
assert Q_TILE == K_TILE
EXIT_BITS = 127.0
NO_TILE_BITS = 16384.0
LOG2_E = 1.4426950408889634

F32 = jnp.float32
BF16 = jnp.bfloat16


def _layer_norm(x, g, b):
    mu = jnp.mean(x, axis=-1, keepdims=True)
    xc = x - mu
    var = jnp.mean(xc * xc, axis=-1, keepdims=True)
    return xc * lax.rsqrt(var + LN_EPS) * g + b


def _slabs(n_rows):
    return [slice(r, r + TOKEN_TILE) for r in range(0, n_rows, TOKEN_TILE)]


def _params(sem):
    return pltpu.CompilerParams(dimension_semantics=sem, vmem_limit_bytes=VMEM_LIMIT)


def _shift_rows(u, prev, shift):
    rolled = pltpu.roll(u, shift, axis=0)
    head_rows = lax.broadcasted_iota(jnp.int32, (SUBLANES, u.shape[1]), 0)
    head = jnp.where(head_rows < shift, pltpu.roll(prev, shift, axis=0), rolled[:SUBLANES])
    return jnp.concatenate([head, rolled[SUBLANES:]], axis=0)


def _mixer_in_kernel(tiles_per_seq, normalize_input, *refs):
    if normalize_input:
        (h_ref, ing_ref, inb_ref, w32_ref, convw_ref, sgw_ref, sgb_ref, lng_ref, lnb_ref,
         h_out_ref, ya_ref, yc_ref, q_ref, k_ref, v_ref, tail_ref, w_ref) = refs
    else:
        (h_ref, w32_ref, convw_ref, sgw_ref, sgb_ref, lng_ref, lnb_ref,
         ya_ref, yc_ref, q_ref, k_ref, v_ref, tail_ref, w_ref) = refs

    @pl.when(pl.program_id(0) == 0)
    def _():
        w_ref[...] = w32_ref[0].astype(BF16)

    @pl.when(pl.program_id(0) % tiles_per_seq == 0)
    def _():
        tail_ref[...] = jnp.zeros_like(tail_ref)

    h = h_ref[...]
    if normalize_input:
        h = _layer_norm(h, ing_ref[...], inb_ref[...])
        h_out_ref[...] = h
    hb = h.astype(BF16)
    proj_c = jnp.dot(hb, w_ref[:, 6 * BRANCH_W:], preferred_element_type=F32)
    proj_a = jnp.dot(hb, w_ref[:, :3 * BRANCH_W], preferred_element_type=F32)
    proj_b = jnp.dot(hb, w_ref[:, 3 * BRANCH_W:6 * BRANCH_W], preferred_element_type=F32)
    col = lambda p, j: p[:, j * BRANCH_W:(j + 1) * BRANCH_W]

    gate_u = jax.nn.gelu(col(proj_c, 0))
    vn = _layer_norm(jax.nn.gelu(col(proj_c, 1)), lng_ref[...], lnb_ref[...]).astype(BF16)
    t_idx = lax.broadcasted_iota(jnp.int32, (CHUNK, CHUNK), 0)
    s_idx = lax.broadcasted_iota(jnp.int32, (CHUNK, CHUNK), 1)
    w_cat = jnp.concatenate(
        [jnp.where(s_idx <= t_idx, sgw_ref[g], 0.0).astype(BF16) for g in range(N_GROUPS_C)], axis=1)
    lane_group = lax.broadcasted_iota(jnp.int32, (1, BRANCH_W), 1) // GROUP_DIM_C
    for c in range(hb.shape[0] // CHUNK):
        rows = slice(c * CHUNK, (c + 1) * CHUNK)
        v_chunk = vn[rows]
        v_groups = jnp.concatenate(
            [jnp.where(lane_group == g, v_chunk, jnp.zeros_like(v_chunk)) for g in range(N_GROUPS_C)],
            axis=0)
        f = jnp.dot(w_cat, v_groups, preferred_element_type=F32) + sgb_ref[...]
        yc_ref[rows, :] = (gate_u[rows] * f).astype(BF16)

    u = col(proj_a, 2) * col(proj_a, 0)
    prev = tail_ref[...]
    conv = (convw_ref[0:1, :] * _shift_rows(u, prev, 2)
            + convw_ref[1:2, :] * _shift_rows(u, prev, 1)
            + convw_ref[2:3, :] * u)
    tail_ref[...] = u[u.shape[0] - SUBLANES:]
    ya_ref[...] = (col(proj_a, 1) * conv).astype(BF16)

    q_ref[...] = (col(proj_b, 0) * (HEAD_DIM ** -0.5 * LOG2_E)).astype(BF16)
    k_ref[...] = col(proj_b, 1).astype(BF16)
    v_ref[...] = col(proj_b, 2).astype(BF16)


def _mixer_in_call(h, w_in, layer, conv_w, sg_w, sg_b_full, sg_ln_g, sg_ln_b, seq_len,
                   input_ln=None):
    t = h.shape[0]
    row = pl.BlockSpec((TOKEN_TILE, D_MODEL), lambda i: (i, 0))
    row_out = pl.BlockSpec((TOKEN_TILE, BRANCH_W), lambda i: (i, 0))
    full = lambda shape: pl.BlockSpec(shape, lambda i: (0,) * len(shape))
    out_sds = jax.ShapeDtypeStruct((t, BRANCH_W), BF16)
    normalize = input_ln is not None
    ln_args = list(input_ln) if normalize else []
    return pl.pallas_call(
        functools.partial(_mixer_in_kernel, seq_len // TOKEN_TILE, normalize),
        grid=(t // TOKEN_TILE,),
        in_specs=[row] + [full((1, D_MODEL))] * len(ln_args) + [
            pl.BlockSpec((1, D_MODEL, MIX_COLS), lambda i: (layer, 0, 0),
                         pipeline_mode=pl.Buffered(1)),
            full((3, BRANCH_W)),
            full((N_GROUPS_C, CHUNK, CHUNK)),
            full((CHUNK, BRANCH_W)),
            full((1, BRANCH_W)),
            full((1, BRANCH_W)),
        ],
        out_specs=[row] * normalize + [row_out] * 5,
        out_shape=[jax.ShapeDtypeStruct((t, D_MODEL), F32)] * normalize + [out_sds] * 5,
        scratch_shapes=[pltpu.VMEM((SUBLANES, BRANCH_W), F32),
                        pltpu.VMEM((D_MODEL, MIX_COLS), BF16)],
        compiler_params=_params(("arbitrary",)),
        name="mixer_in",
    )(h, *ln_args, w_in, conv_w, sg_w, sg_b_full, sg_ln_g, sg_ln_b)


def _attn_kernel(q_ref, k_ref, v_ref, o_ref, acc_ref, spent_ref):
    lane_head = lax.broadcasted_iota(jnp.int32, (1, BRANCH_W), 1) // HEAD_DIM
    t_idx = lax.broadcasted_iota(jnp.int32, (N_HEADS * Q_TILE, K_TILE), 0) % Q_TILE
    s_idx = lax.broadcasted_iota(jnp.int32, (N_HEADS * Q_TILE, K_TILE), 1)
    below_diagonal = s_idx < t_idx
    j_op = lax.broadcasted_iota(jnp.int32, (2 * K_TILE, 2 * K_TILE), 0) % K_TILE
    s_op = lax.broadcasted_iota(jnp.int32, (2 * K_TILE, 2 * K_TILE), 1)
    suffix_op = jnp.where((s_op >= K_TILE) | (j_op > s_op), 1.0, 0.0).astype(BF16)

    def visit(q_stack, kb, spent, diagonal):
        start = pl.multiple_of(jnp.maximum(kb, 0) * K_TILE, K_TILE)
        k_blk = k_ref[0, pl.ds(start, K_TILE), :]
        v_blk = v_ref[0, pl.ds(start, K_TILE), :]
        z = lax.dot_general(q_stack, k_blk, (((1,), (1,)), ((), ())), preferred_element_type=F32)
        softplus = jnp.maximum(z, 0.0) + jnp.log2(1.0 + jnp.exp2(-jnp.abs(z)))
        if diagonal:
            softplus = jnp.where(below_diagonal, softplus, 0.0)
        else:
            spent = spent + jnp.where(kb >= 0, 0.0, NO_TILE_BITS)
        hi = softplus.astype(BF16)
        lo = (softplus - hi.astype(F32)).astype(BF16)
        sums = jnp.dot(jnp.concatenate([hi, lo], axis=1), suffix_op, preferred_element_type=F32)
        a = jnp.exp2(z - softplus - (sums[:, :K_TILE] + spent))
        if diagonal:
            a = jnp.where(below_diagonal, a, 0.0)
        a = a.astype(BF16)
        a_heads = jnp.concatenate([a[h * Q_TILE:(h + 1) * Q_TILE] for h in range(N_HEADS)], axis=1)
        v_heads = jnp.concatenate(
            [jnp.where(lane_head == h, v_blk, jnp.zeros_like(v_blk)) for h in range(N_HEADS)], axis=0)
        out = jnp.dot(a_heads, v_heads, preferred_element_type=F32)
        return spent + sums[:, K_TILE:], out

    def stacked_queries(sub):
        q = q_ref[0, sub * Q_TILE:(sub + 1) * Q_TILE, :]
        return jnp.concatenate(
            [jnp.where(lane_head == h, q, jnp.zeros_like(q)) for h in range(N_HEADS)], axis=0)

    first_tile = pl.program_id(1) * Q_SUBTILES
    least = []
    for sub in range(Q_SUBTILES):
        tile = first_tile + sub
        q_stack = stacked_queries(sub)
        spent, out0 = visit(q_stack, tile, jnp.zeros((N_HEADS * Q_TILE, K_TILE), F32), True)
        spent, out1 = visit(q_stack, tile - 1, spent, False)
        acc_ref[sub] = out0 + out1
        spent_ref[sub] = spent
        least.append(jnp.min(spent))

    def walk(state):
        step, least = state

        def advance(sub):
            spent, out = visit(stacked_queries(sub), first_tile + sub - step, spent_ref[sub], False)
            acc_ref[sub] += out
            spent_ref[sub] = spent
            return jnp.min(spent)

        return step + 1, tuple(
            lax.cond(least[sub] < EXIT_BITS, functools.partial(advance, sub), lambda sub=sub: least[sub])
            for sub in range(Q_SUBTILES))

    def unfinished(state):
        _, least = state
        return functools.reduce(jnp.minimum, least) < EXIT_BITS

    lax.while_loop(unfinished, walk, (jnp.int32(2), tuple(least)))
    for sub in range(Q_SUBTILES):
        o_ref[0, sub * Q_TILE:(sub + 1) * Q_TILE, :] = acc_ref[sub].astype(BF16)


def _attn_call(q, k, v):
    b, s, _ = q.shape
    rows = Q_SUBTILES * Q_TILE
    tile = pl.BlockSpec((1, rows, BRANCH_W), lambda bi, qi: (bi, qi, 0))
    seq = pl.BlockSpec((1, s, BRANCH_W), lambda bi, qi: (bi, 0, 0))
    return pl.pallas_call(
        _attn_kernel,
        grid=(b, s // rows),
        in_specs=[tile, seq, seq],
        out_specs=tile,
        out_shape=jax.ShapeDtypeStruct((b, s, BRANCH_W), BF16),
        scratch_shapes=[pltpu.VMEM((Q_SUBTILES, Q_TILE, BRANCH_W), F32),
                        pltpu.VMEM((Q_SUBTILES, N_HEADS * Q_TILE, K_TILE), F32)],
        compiler_params=_params(("parallel", "parallel")),
        name="attn",
    )(q, k, v)


def _merge_kernel(h_ref, ya_ref, yb_ref, yc_ref, wga32_ref, wgb32_ref, wgc32_ref, bgate_ref,
                  wbr32_ref, wo32_ref, g_ref, b_ref, o_ref, wg_ref, wbr_ref, wo_ref):
    @pl.when(pl.program_id(0) == 0)
    def _():
        for i, w32_ref in enumerate((wga32_ref, wgb32_ref, wgc32_ref)):
            wg_ref[:, i * D_MODEL:(i + 1) * D_MODEL] = w32_ref[0].astype(BF16)
        wbr_ref[...] = wbr32_ref[0].astype(BF16)
        wo_ref[...] = wo32_ref[0].astype(BF16)

    for rows in _slabs(h_ref.shape[0]):
        h = h_ref[rows, :]
        logits = jnp.dot(h.astype(BF16), wg_ref[...], preferred_element_type=F32)
        merged = None
        for i, y_ref in enumerate((ya_ref, yb_ref, yc_ref)):
            gate = 0.5 + 0.5 * jnp.tanh(
                0.5 * (logits[:, i * D_MODEL:(i + 1) * D_MODEL] + bgate_ref[i:i + 1, :]))
            term = gate * jnp.dot(y_ref[rows, :], wbr_ref[i], preferred_element_type=F32)
            merged = term if merged is None else merged + term
        mix = jnp.dot(merged.astype(BF16), wo_ref[...], preferred_element_type=F32)
        o_ref[rows, :] = _layer_norm(DEEPNORM_ALPHA * h + mix, g_ref[...], b_ref[...])


def _merge_call(h, y_a, y_b, y_c, w_in, w_br, w_o, layer, b_gate, ln_g, ln_b):
    t = h.shape[0]
    row = pl.BlockSpec((CHANNEL_TILE, D_MODEL), lambda i: (i, 0))
    branch = pl.BlockSpec((CHANNEL_TILE, BRANCH_W), lambda i: (i, 0))
    full = lambda shape: pl.BlockSpec(shape, lambda i: (0,) * len(shape))
    resident = lambda shape, index: pl.BlockSpec(shape, index, pipeline_mode=pl.Buffered(1))
    gate_cols = [resident((1, D_MODEL, D_MODEL), lambda i, j=j: (layer, 0, MIX_COLS // D_MODEL + j))
                 for j in range(N_BRANCH)]
    return pl.pallas_call(
        _merge_kernel,
        grid=(t // CHANNEL_TILE,),
        in_specs=[row, branch, branch, branch, *gate_cols,
                  full((N_BRANCH, D_MODEL)),
                  resident((1, N_BRANCH, BRANCH_W, D_MODEL), lambda i: (layer, 0, 0, 0)),
                  resident((1, D_MODEL, D_MODEL), lambda i: (layer, 0, 0)),
                  full((1, D_MODEL)), full((1, D_MODEL))],
        out_specs=row,
        out_shape=jax.ShapeDtypeStruct((t, D_MODEL), F32),
        scratch_shapes=[pltpu.VMEM((D_MODEL, N_BRANCH * D_MODEL), BF16),
                        pltpu.VMEM((N_BRANCH, BRANCH_W, D_MODEL), BF16),
                        pltpu.VMEM((D_MODEL, D_MODEL), BF16)],
        compiler_params=_params(("arbitrary",)),
        name="merge",
    )(h, y_a, y_b, y_c, w_in, w_in, w_in, b_gate, w_br, w_o, ln_g, ln_b)


def _top2_of_group(sel, scores):
    def first_argmax(vals):
        top = functools.reduce(jnp.maximum, vals)
        pos = jnp.full(top.shape, len(vals) - 1, jnp.int32)
        for j in reversed(range(len(vals) - 1)):
            pos = jnp.where(vals[j] == top, j, pos)
        return top, pos

    def pick(rows, pos):
        out = rows[-1]
        for j in reversed(range(len(rows) - 1)):
            out = jnp.where(pos == j, rows[j], out)
        return out

    m1, i1 = first_argmax(sel)
    m2, i2 = first_argmax([jnp.where(i1 == j, -jnp.inf, v) for j, v in enumerate(sel)])
    return m1 + m2, i1, i2, pick(scores, i1), pick(scores, i2)


def _load_expert_weights(layer, wg_hbm, wu_hbm, wd_hbm, wg_ref, wu_ref, wd_ref,
                         stage_in_ref, stage_out_ref, sem):
    jobs = []
    for src_hbm, dst_ref in ((wg_hbm, wg_ref), (wu_hbm, wu_ref)):
        for e in range(N_EXPERTS):
            jobs.append((src_hbm.at[layer, e], stage_in_ref,
                         dst_ref.at[:, e * D_FF_EXPERT:(e + 1) * D_FF_EXPERT]))
    for e in range(N_EXPERTS):
        jobs.append((wd_hbm.at[layer, e], stage_out_ref,
                     wd_ref.at[e * D_FF_EXPERT:(e + 1) * D_FF_EXPERT, :]))

    def copy(i):
        src, stage_ref, _ = jobs[i]
        return pltpu.make_async_copy(src, stage_ref.at[i % 2], sem.at[i % 2])

    copy(0).start()
    for i, (_, stage_ref, dst) in enumerate(jobs):
        if i + 1 < len(jobs):
            copy(i + 1).start()
        copy(i).wait()
        dst[...] = stage_ref[i % 2].astype(BF16)


def _moe_kernel(layer, x_ref, wr_ref, rb_ref, prefix_ref, wg_hbm, wu_hbm, wd_hbm, g_ref, b_ref,
                o_ref, rows_ref, xs_ref, ws_ref, ys_ref, unsort_ref, wg_ref, wu_ref, wd_ref,
                stage_in_ref, stage_out_ref, sem):
    @pl.when(pl.program_id(0) == 0)
    def _():
        _load_expert_weights(layer, wg_hbm, wu_hbm, wd_hbm, wg_ref, wu_ref, wd_ref,
                             stage_in_ref, stage_out_ref, sem)

    slabs = range(x_ref.shape[0] // TOKEN_TILE)
    group_cols = EXPERTS_PER_GROUP * D_FF_EXPERT

    def piece(slab, starts, g, c):
        cols_g = slice(g * group_cols, (g + 1) * group_cols)
        r0 = pl.multiple_of(starts[g] // SORT_ALIGN * SORT_ALIGN + c * SORT_CHUNK, SORT_ALIGN)
        xs = xs_ref[slab, pl.ds(r0, SORT_CHUNK), :]
        gate = jnp.dot(xs, wg_ref[:, cols_g], preferred_element_type=F32)
        up = jnp.dot(xs, wu_ref[:, cols_g], preferred_element_type=F32)
        wc = ws_ref[slab, pl.ds(r0, SORT_CHUNK), :]
        w_cols = jnp.concatenate(
            [jnp.broadcast_to(wc[:, e:e + 1], (SORT_CHUNK, D_FF_EXPERT))
             for e in range(g * EXPERTS_PER_GROUP, (g + 1) * EXPERTS_PER_GROUP)], axis=1)
        hidden = (jax.nn.silu(gate) * up * w_cols).astype(BF16)
        ys_ref[slab, pl.ds(r0, SORT_CHUNK), :] += jnp.dot(hidden, wd_ref[cols_g, :],
                                                           preferred_element_type=F32)

    layout = []
    for slab in slabs:
        counts, starts = _moe_route_and_sort(slab, x_ref, wr_ref, rb_ref, prefix_ref,
                                             rows_ref, xs_ref, ws_ref, ys_ref, unsort_ref)
        for g in range(N_EXPERT_GROUPS):
            piece(slab, starts, g, 0)
        layout.append((counts, starts))

    for slab in slabs:
        counts, starts = layout[slab]
        for g in range(N_EXPERT_GROUPS):
            span = counts[g] + starts[g] % SORT_ALIGN
            lax.fori_loop(1, (span + (SORT_CHUNK - 1)) // SORT_CHUNK,
                          lambda c, carry, slab=slab, starts=starts, g=g:
                          (piece(slab, starts, g, c), carry)[1], 0)

    for slab in slabs:
        rows = slice(slab * TOKEN_TILE, (slab + 1) * TOKEN_TILE)
        y = jnp.dot(unsort_ref[slab], ys_ref[slab, :TOKEN_TILE, :].astype(BF16),
                    preferred_element_type=F32)
        o_ref[rows, :] = _layer_norm(DEEPNORM_ALPHA * x_ref[rows, :] + y, g_ref[...], b_ref[...])


def _moe_route_and_sort(slab, x_ref, wr_ref, rb_ref, prefix_ref,
                        rows_ref, xs_ref, ws_ref, ys_ref, unsort_ref):
    n_tok = TOKEN_TILE
    rows_ref = rows_ref.at[slab]
    x = x_ref[slab * TOKEN_TILE:(slab + 1) * TOKEN_TILE, :]
    xb = x.astype(BF16)
    x_lo = (x - xb.astype(F32)).astype(BF16)
    r = jnp.dot(xb, wr_ref[...], preferred_element_type=F32)
    logits = (r[:, :LANES] + r[:, LANES:]
              + jnp.dot(x_lo, wr_ref[:, :LANES], preferred_element_type=F32))

    scores_t = jax.nn.sigmoid(logits.T[:N_EXPERTS])
    rows_ref[:N_EXPERTS, :] = scores_t
    rows_ref[N_EXPERTS:2 * N_EXPERTS, :] = scores_t + rb_ref[...]
    score_rows = [rows_ref[e:e + 1, :] for e in range(N_EXPERTS)]
    sel_rows = [rows_ref[N_EXPERTS + e:N_EXPERTS + e + 1, :] for e in range(N_EXPERTS)]
    best = None
    for g in range(N_EXPERT_GROUPS):
        members = slice(g * EXPERTS_PER_GROUP, (g + 1) * EXPERTS_PER_GROUP)
        cand = _top2_of_group(sel_rows[members], score_rows[members])
        cand = cand + (jnp.full(cand[1].shape, g, jnp.int32),)
        if best is None:
            best = cand
        else:
            better = cand[0] > best[0]
            best = tuple(jnp.where(better, c, o) for c, o in zip(cand, best))
    _, i1, i2, s1, s2, group = best
    denom = s1 + s2
    w1, w2 = s1 / denom, s2 / denom
    dense_rows = [
        jnp.where(group == e // EXPERTS_PER_GROUP,
                  jnp.where(i1 == e % EXPERTS_PER_GROUP, w1, 0.0)
                  + jnp.where(i2 == e % EXPERTS_PER_GROUP, w2, 0.0), 0.0)
        for e in range(N_EXPERTS)]

    in_group = [jnp.where(group == g, 1.0, 0.0) for g in range(N_EXPERT_GROUPS)]
    rows_ref[...] = jnp.zeros_like(rows_ref)
    for g in range(N_EXPERT_GROUPS):
        rows_ref[g:g + 1, :] = in_group[g]
    onehot_t = rows_ref[:2 * SUBLANES, :].astype(BF16)
    earlier = jnp.dot(onehot_t, prefix_ref[...], preferred_element_type=F32)
    counts = [jnp.sum(m).astype(jnp.int32) for m in in_group]
    starts, nxt = [], jnp.int32(0)
    for g in range(N_EXPERT_GROUPS):
        starts.append(nxt)
        nxt = nxt + counts[g]
    dest_t = sum(in_group[g] * (earlier[g:g + 1] + starts[g].astype(F32))
                 for g in range(N_EXPERT_GROUPS))

    for e in range(N_EXPERTS):
        rows_ref[e:e + 1, :] = dense_rows[e]
    rows_ref[N_EXPERTS:N_EXPERTS + 1, :] = dest_t
    cols = rows_ref[...].T
    dest_col = cols[:, N_EXPERTS:N_EXPERTS + 1].astype(jnp.int32)
    to_sorted = jnp.where(
        lax.broadcasted_iota(jnp.int32, (n_tok, n_tok), 0) == dest_t.astype(jnp.int32),
        1.0, 0.0).astype(BF16)
    unsort_ref[slab] = jnp.where(
        lax.broadcasted_iota(jnp.int32, (n_tok, n_tok), 1) == dest_col, 1.0, 0.0).astype(BF16)

    xs_ref[slab, :n_tok, :] = jnp.dot(to_sorted, xb, preferred_element_type=F32).astype(BF16)
    xs_ref[slab, n_tok:, :] = jnp.zeros((SORT_CHUNK, D_MODEL), BF16)
    w_hi = cols.astype(BF16)
    w_split = jnp.concatenate([w_hi, (cols - w_hi.astype(F32)).astype(BF16)], axis=1)
    w_sorted = jnp.dot(to_sorted, w_split, preferred_element_type=F32)
    ws_ref[slab, :n_tok, :] = w_sorted[:, :LANES] + w_sorted[:, LANES:]
    ws_ref[slab, n_tok:, :] = jnp.zeros((SORT_CHUNK, LANES), F32)
    ys_ref[slab] = jnp.zeros(ys_ref.shape[1:], F32)
    return counts, starts


def _moe_call(x, wr_split, rb_col, prefix_op, w_gate, w_up, w_down, layer, ln_g, ln_b):
    t = x.shape[0]
    row = pl.BlockSpec((CHANNEL_TILE, D_MODEL), lambda i: (i, 0))
    full = lambda shape: pl.BlockSpec(shape, lambda i: (0,) * len(shape))
    in_hbm = pl.BlockSpec(memory_space=pl.ANY)
    n_slabs = CHANNEL_TILE // TOKEN_TILE
    return pl.pallas_call(
        functools.partial(_moe_kernel, layer),
        grid=(t // CHANNEL_TILE,),
        in_specs=[row,
                  full((D_MODEL, 2 * LANES)), full((N_EXPERTS, 1)), full((TOKEN_TILE, TOKEN_TILE)),
                  in_hbm, in_hbm, in_hbm,
                  full((1, D_MODEL)), full((1, D_MODEL))],
        out_specs=row,
        out_shape=jax.ShapeDtypeStruct((t, D_MODEL), F32),
        scratch_shapes=[pltpu.VMEM((n_slabs, LANES, TOKEN_TILE), F32),
                        pltpu.VMEM((n_slabs, TOKEN_TILE + SORT_CHUNK, D_MODEL), BF16),
                        pltpu.VMEM((n_slabs, TOKEN_TILE + SORT_CHUNK, LANES), F32),
                        pltpu.VMEM((n_slabs, TOKEN_TILE + SORT_CHUNK, D_MODEL), F32),
                        pltpu.VMEM((n_slabs, TOKEN_TILE, TOKEN_TILE), BF16),
                        pltpu.VMEM((D_MODEL, FF_ALL), BF16),
                        pltpu.VMEM((D_MODEL, FF_ALL), BF16),
                        pltpu.VMEM((FF_ALL, D_MODEL), BF16),
                        pltpu.VMEM((2, D_MODEL, D_FF_EXPERT), F32),
                        pltpu.VMEM((2, D_FF_EXPERT, D_MODEL), F32),
                        pltpu.SemaphoreType.DMA((2,))],
        compiler_params=_params(("arbitrary",)),
        name="moe",
    )(x, wr_split, rb_col, prefix_op, w_gate, w_up, w_down, ln_g, ln_b)


def kernel(x, ln_in_g, ln_in_b, w_in, b_gate, conv_w, sg_w, sg_b, sg_ln_g, sg_ln_b, w_br, w_o,
           ln_mix_g, ln_mix_b, w_router, router_bias, w_gate, w_up, w_down, ln_ffn_g, ln_ffn_b):
    batch, seq_len, d = x.shape
    assert d == D_MODEL and seq_len % TOKEN_TILE == 0 and seq_len % (Q_SUBTILES * Q_TILE) == 0
    t = batch * seq_len
    vec = lambda a: a.reshape(1, -1)

    wr_pad = jnp.pad(w_router, ((0, 0), (0, LANES - N_EXPERTS)))
    wr_hi = wr_pad.astype(BF16)
    wr_split = jnp.concatenate([wr_hi, (wr_pad - wr_hi.astype(F32)).astype(BF16)], axis=1)
    rb_col = router_bias.reshape(N_EXPERTS, 1)
    prefix_op = (lax.broadcasted_iota(jnp.int32, (TOKEN_TILE, TOKEN_TILE), 0)
                 < lax.broadcasted_iota(jnp.int32, (TOKEN_TILE, TOKEN_TILE), 1)).astype(BF16)

    h = x.reshape(t, d)
    for l in range(DEPTH):
        sg_b_full = jnp.repeat(sg_b[l].T, GROUP_DIM_C, axis=1)
        mixer_args = (w_in, l, conv_w[l], sg_w[l], sg_b_full, vec(sg_ln_g[l]), vec(sg_ln_b[l]),
                      seq_len)
        if l == 0:
            h, y_a, y_c, q, k, v = _mixer_in_call(h, *mixer_args,
                                                  input_ln=(vec(ln_in_g), vec(ln_in_b)))
        else:
            y_a, y_c, q, k, v = _mixer_in_call(h, *mixer_args)
        shape3 = (batch, seq_len, BRANCH_W)
        y_b = _attn_call(q.reshape(shape3), k.reshape(shape3), v.reshape(shape3)).reshape(t, BRANCH_W)
        h = _merge_call(h, y_a, y_b, y_c, w_in, w_br, w_o, l, b_gate[l],
                        vec(ln_mix_g[l]), vec(ln_mix_b[l]))
        h = _moe_call(h, wr_split, rb_col, prefix_op, w_gate, w_up, w_down, l,
                      vec(ln_ffn_g[l]), vec(ln_ffn_b[l]))
    return h.reshape(batch, seq_len, d)
```

```python
import functools

import jax
import jax.numpy as jnp
from jax import lax
from jax.experimental import pallas as pl
from jax.experimental.pallas import tpu as pltpu

D_MODEL = 1024
DEPTH = 4
BRANCH_W = 256
N_HEADS = 4
HEAD_DIM = 64
CHUNK = 128
N_GROUPS_C = 4
GROUP_DIM_C = 64
MIX_COLS = 8 * BRANCH_W
N_BRANCH = 3
N_EXPERTS = 16
EXPERTS_PER_GROUP = 4
N_EXPERT_GROUPS = 4
D_FF_EXPERT = 128
FF_ALL = N_EXPERTS * D_FF_EXPERT
DEEPNORM_ALPHA = (2 * DEPTH) ** 0.25
LN_EPS = 1e-5

LANES = 128
SUBLANES = 8
VMEM_LIMIT = 56 * 1024 * 1024

TOKEN_TILE = 512
CHANNEL_TILE = 1024
SORT_ALIGN = 16
SORT_CHUNK = 10 * SORT_ALIGN
Q_TILE = 128
K_TILE = 128
Q_SUBTILES = 8
assert Q_TILE == K_TILE
EXIT_BITS = 127.0
NO_TILE_BITS = 16384.0
LOG2_E = 1.4426950408889634

F32 = jnp.float32
BF16 = jnp.bfloat16


def _layer_norm(x, g, b):
    mu = jnp.mean(x, axis=-1, keepdims=True)
    xc = x - mu
    var = jnp.mean(xc * xc, axis=-1, keepdims=True)
    return xc * lax.rsqrt(var + LN_EPS) * g + b


def _slabs(n_rows):
    return [slice(r, r + TOKEN_TILE) for r in range(0, n_rows, TOKEN_TILE)]


def _params(sem):
    return pltpu.CompilerParams(dimension_semantics=sem, vmem_limit_bytes=VMEM_LIMIT)


def _shift_rows(u, prev, shift):
    rolled = pltpu.roll(u, shift, axis=0)
    head_rows = lax.broadcasted_iota(jnp.int32, (SUBLANES, u.shape[1]), 0)
    head = jnp.where(head_rows < shift, pltpu.roll(prev, shift, axis=0), rolled[:SUBLANES])
    return jnp.concatenate([head, rolled[SUBLANES:]], axis=0)


def _mixer_in_kernel(tiles_per_seq, normalize_input, *refs):
    if normalize_input:
        (h_ref, ing_ref, inb_ref, w32_ref, convw_ref, sgw_ref, sgb_ref, lng_ref, lnb_ref,
         h_out_ref, ya_ref, yc_ref, q_ref, k_ref, v_ref, tail_ref, w_ref) = refs
    else:
        (h_ref, w32_ref, convw_ref, sgw_ref, sgb_ref, lng_ref, lnb_ref,
         ya_ref, yc_ref, q_ref, k_ref, v_ref, tail_ref, w_ref) = refs

    @pl.when(pl.program_id(0) == 0)
    def _():
        w_ref[...] = w32_ref[0].astype(BF16)

    @pl.when(pl.program_id(0) % tiles_per_seq == 0)
    def _():
        tail_ref[...] = jnp.zeros_like(tail_ref)

    h = h_ref[...]
    if normalize_input:
        h = _layer_norm(h, ing_ref[...], inb_ref[...])
        h_out_ref[...] = h
    hb = h.astype(BF16)
    proj_c = jnp.dot(hb, w_ref[:, 6 * BRANCH_W:], preferred_element_type=F32)
    proj_a = jnp.dot(hb, w_ref[:, :3 * BRANCH_W], preferred_element_type=F32)
    proj_b = jnp.dot(hb, w_ref[:, 3 * BRANCH_W:6 * BRANCH_W], preferred_element_type=F32)
    col = lambda p, j: p[:, j * BRANCH_W:(j + 1) * BRANCH_W]

    gate_u = jax.nn.gelu(col(proj_c, 0))
    vn = _layer_norm(jax.nn.gelu(col(proj_c, 1)), lng_ref[...], lnb_ref[...]).astype(BF16)
    t_idx = lax.broadcasted_iota(jnp.int32, (CHUNK, CHUNK), 0)
    s_idx = lax.broadcasted_iota(jnp.int32, (CHUNK, CHUNK), 1)
    w_cat = jnp.concatenate(
        [jnp.where(s_idx <= t_idx, sgw_ref[g], 0.0).astype(BF16) for g in range(N_GROUPS_C)], axis=1)
    lane_group = lax.broadcasted_iota(jnp.int32, (1, BRANCH_W), 1) // GROUP_DIM_C
    for c in range(hb.shape[0] // CHUNK):
        rows = slice(c * CHUNK, (c + 1) * CHUNK)
        v_chunk = vn[rows]
        v_groups = jnp.concatenate(
            [jnp.where(lane_group == g, v_chunk, jnp.zeros_like(v_chunk)) for g in range(N_GROUPS_C)],
            axis=0)
        f = jnp.dot(w_cat, v_groups, preferred_element_type=F32) + sgb_ref[...]
        yc_ref[rows, :] = (gate_u[rows] * f).astype(BF16)

    u = col(proj_a, 2) * col(proj_a, 0)
    prev = tail_ref[...]
    conv = (convw_ref[0:1, :] * _shift_rows(u, prev, 2)
            + convw_ref[1:2, :] * _shift_rows(u, prev, 1)
            + convw_ref[2:3, :] * u)
    tail_ref[...] = u[u.shape[0] - SUBLANES:]
    ya_ref[...] = (col(proj_a, 1) * conv).astype(BF16)

    q_ref[...] = (col(proj_b, 0) * (HEAD_DIM ** -0.5 * LOG2_E)).astype(BF16)
    k_ref[...] = col(proj_b, 1).astype(BF16)
    v_ref[...] = col(proj_b, 2).astype(BF16)


def _mixer_in_call(h, w_in, layer, conv_w, sg_w, sg_b_full, sg_ln_g, sg_ln_b, seq_len,
                   input_ln=None):
    t = h.shape[0]
    row = pl.BlockSpec((TOKEN_TILE, D_MODEL), lambda i: (i, 0))
    row_out = pl.BlockSpec((TOKEN_TILE, BRANCH_W), lambda i: (i, 0))
    full = lambda shape: pl.BlockSpec(shape, lambda i: (0,) * len(shape))
    out_sds = jax.ShapeDtypeStruct((t, BRANCH_W), BF16)
    normalize = input_ln is not None
    ln_args = list(input_ln) if normalize else []
    return pl.pallas_call(
        functools.partial(_mixer_in_kernel, seq_len // TOKEN_TILE, normalize),
        grid=(t // TOKEN_TILE,),
        in_specs=[row] + [full((1, D_MODEL))] * len(ln_args) + [
            pl.BlockSpec((1, D_MODEL, MIX_COLS), lambda i: (layer, 0, 0),
                         pipeline_mode=pl.Buffered(1)),
            full((3, BRANCH_W)),
            full((N_GROUPS_C, CHUNK, CHUNK)),
            full((CHUNK, BRANCH_W)),
            full((1, BRANCH_W)),
            full((1, BRANCH_W)),
        ],
        out_specs=[row] * normalize + [row_out] * 5,
        out_shape=[jax.ShapeDtypeStruct((t, D_MODEL), F32)] * normalize + [out_sds] * 5,
        scratch_shapes=[pltpu.VMEM((SUBLANES, BRANCH_W), F32),
                        pltpu.VMEM((D_MODEL, MIX_COLS), BF16)],
        compiler_params=_params(("arbitrary",)),
        name="mixer_in",
    )(h, *ln_args, w_in, conv_w, sg_w, sg_b_full, sg_ln_g, sg_ln_b)


def _attn_kernel(q_ref, k_ref, v_ref, o_ref, acc_ref, spent_ref):
    lane_head = lax.broadcasted_iota(jnp.int32, (1, BRANCH_W), 1) // HEAD_DIM
    t_idx = lax.broadcasted_iota(jnp.int32, (N_HEADS * Q_TILE, K_TILE), 0) % Q_TILE
    s_idx = lax.broadcasted_iota(jnp.int32, (N_HEADS * Q_TILE, K_TILE), 1)
    below_diagonal = s_idx < t_idx
    j_op = lax.broadcasted_iota(jnp.int32, (2 * K_TILE, 2 * K_TILE), 0) % K_TILE
    s_op = lax.broadcasted_iota(jnp.int32, (2 * K_TILE, 2 * K_TILE), 1)
    suffix_op = jnp.where((s_op >= K_TILE) | (j_op > s_op), 1.0, 0.0).astype(BF16)

    def visit(q_stack, kb, spent, diagonal):
        start = pl.multiple_of(jnp.maximum(kb, 0) * K_TILE, K_TILE)
        k_blk = k_ref[0, pl.ds(start, K_TILE), :]
        v_blk = v_ref[0, pl.ds(start, K_TILE), :]
        z = lax.dot_general(q_stack, k_blk, (((1,), (1,)), ((), ())), preferred_element_type=F32)
        softplus = jnp.maximum(z, 0.0) + jnp.log2(1.0 + jnp.exp2(-jnp.abs(z)))
        if diagonal:
            softplus = jnp.where(below_diagonal, softplus, 0.0)
        else:
            spent = spent + jnp.where(kb >= 0, 0.0, NO_TILE_BITS)
        hi = softplus.astype(BF16)
        lo = (softplus - hi.astype(F32)).astype(BF16)
        sums = jnp.dot(jnp.concatenate([hi, lo], axis=1), suffix_op, preferred_element_type=F32)
        a = jnp.exp2(z - softplus - (sums[:, :K_TILE] + spent))
        if diagonal:
            a = jnp.where(below_diagonal, a, 0.0)
        a = a.astype(BF16)
        a_heads = jnp.concatenate([a[h * Q_TILE:(h + 1) * Q_TILE] for h in range(N_HEADS)], axis=1)
        v_heads = jnp.concatenate(
            [jnp.where(lane_head == h, v_blk, jnp.zeros_like(v_blk)) for h in range(N_HEADS)], axis=0)
        out = jnp.dot(a_heads, v_heads, preferred_element_type=F32)
        return spent + sums[:, K_TILE:], out

    def stacked_queries(sub):
        q = q_ref[0, sub * Q_TILE:(sub + 1) * Q_TILE, :]
        return jnp.concatenate(
            [jnp.where(lane_head == h, q, jnp.zeros_like(q)) for h in range(N_HEADS)], axis=0)

    first_tile = pl.program_id(1) * Q_SUBTILES
    least = []
    for sub in range(Q_SUBTILES):
        tile = first_tile + sub
        q_stack = stacked_queries(sub)
        spent, out0 = visit(q_stack, tile, jnp.zeros((N_HEADS * Q_TILE, K_TILE), F32), True)
        spent, out1 = visit(q_stack, tile - 1, spent, False)
        acc_ref[sub] = out0 + out1
        spent_ref[sub] = spent
        least.append(jnp.min(spent))

    def walk(state):
        step, least = state

        def advance(sub):
            spent, out = visit(stacked_queries(sub), first_tile + sub - step, spent_ref[sub], False)
            acc_ref[sub] += out
            spent_ref[sub] = spent
            return jnp.min(spent)

        return step + 1, tuple(
            lax.cond(least[sub] < EXIT_BITS, functools.partial(advance, sub), lambda sub=sub: least[sub])
            for sub in range(Q_SUBTILES))

    def unfinished(state):
        _, least = state
        return functools.reduce(jnp.minimum, least) < EXIT_BITS

    lax.while_loop(unfinished, walk, (jnp.int32(2), tuple(least)))
    for sub in range(Q_SUBTILES):
        o_ref[0, sub * Q_TILE:(sub + 1) * Q_TILE, :] = acc_ref[sub].astype(BF16)


def _attn_call(q, k, v):
    b, s, _ = q.shape
    rows = Q_SUBTILES * Q_TILE
    tile = pl.BlockSpec((1, rows, BRANCH_W), lambda bi, qi: (bi, qi, 0))
    seq = pl.BlockSpec((1, s, BRANCH_W), lambda bi, qi: (bi, 0, 0))
    return pl.pallas_call(
        _attn_kernel,
        grid=(b, s // rows),
        in_specs=[tile, seq, seq],
        out_specs=tile,
        out_shape=jax.ShapeDtypeStruct((b, s, BRANCH_W), BF16),
        scratch_shapes=[pltpu.VMEM((Q_SUBTILES, Q_TILE, BRANCH_W), F32),
                        pltpu.VMEM((Q_SUBTILES, N_HEADS * Q_TILE, K_TILE), F32)],
        compiler_params=_params(("parallel", "parallel")),
        name="attn",
    )(q, k, v)


def _merge_kernel(h_ref, ya_ref, yb_ref, yc_ref, wga32_ref, wgb32_ref, wgc32_ref, bgate_ref,
                  wbr32_ref, wo32_ref, g_ref, b_ref, o_ref, wg_ref, wbr_ref, wo_ref):
    @pl.when(pl.program_id(0) == 0)
    def _():
        for i, w32_ref in enumerate((wga32_ref, wgb32_ref, wgc32_ref)):
            wg_ref[:, i * D_MODEL:(i + 1) * D_MODEL] = w32_ref[0].astype(BF16)
        wbr_ref[...] = wbr32_ref[0].astype(BF16)
        wo_ref[...] = wo32_ref[0].astype(BF16)

    for rows in _slabs(h_ref.shape[0]):
        h = h_ref[rows, :]
        logits = jnp.dot(h.astype(BF16), wg_ref[...], preferred_element_type=F32)
        merged = None
        for i, y_ref in enumerate((ya_ref, yb_ref, yc_ref)):
            gate = 0.5 + 0.5 * jnp.tanh(
                0.5 * (logits[:, i * D_MODEL:(i + 1) * D_MODEL] + bgate_ref[i:i + 1, :]))
            term = gate * jnp.dot(y_ref[rows, :], wbr_ref[i], preferred_element_type=F32)
            merged = term if merged is None else merged + term
        mix = jnp.dot(merged.astype(BF16), wo_ref[...], preferred_element_type=F32)
        o_ref[rows, :] = _layer_norm(DEEPNORM_ALPHA * h + mix, g_ref[...], b_ref[...])


def _merge_call(h, y_a, y_b, y_c, w_in, w_br, w_o, layer, b_gate, ln_g, ln_b):
    t = h.shape[0]
    row = pl.BlockSpec((CHANNEL_TILE, D_MODEL), lambda i: (i, 0))
    branch = pl.BlockSpec((CHANNEL_TILE, BRANCH_W), lambda i: (i, 0))
    full = lambda shape: pl.BlockSpec(shape, lambda i: (0,) * len(shape))
    resident = lambda shape, index: pl.BlockSpec(shape, index, pipeline_mode=pl.Buffered(1))
    gate_cols = [resident((1, D_MODEL, D_MODEL), lambda i, j=j: (layer, 0, MIX_COLS // D_MODEL + j))
                 for j in range(N_BRANCH)]
    return pl.pallas_call(
        _merge_kernel,
        grid=(t // CHANNEL_TILE,),
        in_specs=[row, branch, branch, branch, *gate_cols,
                  full((N_BRANCH, D_MODEL)),
                  resident((1, N_BRANCH, BRANCH_W, D_MODEL), lambda i: (layer, 0, 0, 0)),
                  resident((1, D_MODEL, D_MODEL), lambda i: (layer, 0, 0)),
                  full((1, D_MODEL)), full((1, D_MODEL))],
        out_specs=row,
        out_shape=jax.ShapeDtypeStruct((t, D_MODEL), F32),
        scratch_shapes=[pltpu.VMEM((D_MODEL, N_BRANCH * D_MODEL), BF16),
                        pltpu.VMEM((N_BRANCH, BRANCH_W, D_MODEL), BF16),
                        pltpu.VMEM((D_MODEL, D_MODEL), BF16)],
        compiler_params=_params(("arbitrary",)),
        name="merge",
    )(h, y_a, y_b, y_c, w_in, w_in, w_in, b_gate, w_br, w_o, ln_g, ln_b)


def _top2_of_group(sel, scores):
    def first_argmax(vals):
        top = functools.reduce(jnp.maximum, vals)
        pos = jnp.full(top.shape, len(vals) - 1, jnp.int32)
        for j in reversed(range(len(vals) - 1)):
            pos = jnp.where(vals[j] == top, j, pos)
        return top, pos

    def pick(rows, pos):
        out = rows[-1]
        for j in reversed(range(len(rows) - 1)):
            out = jnp.where(pos == j, rows[j], out)
        return out

    m1, i1 = first_argmax(sel)
    m2, i2 = first_argmax([jnp.where(i1 == j, -jnp.inf, v) for j, v in enumerate(sel)])
    return m1 + m2, i1, i2, pick(scores, i1), pick(scores, i2)


def _load_expert_weights(layer, wg_hbm, wu_hbm, wd_hbm, wg_ref, wu_ref, wd_ref,
                         stage_in_ref, stage_out_ref, sem):
    expert_cols = lambda e: slice(e * D_FF_EXPERT, (e + 1) * D_FF_EXPERT)
    jobs = []
    for g in range(N_EXPERT_GROUPS):
        experts = range(g * EXPERTS_PER_GROUP, (g + 1) * EXPERTS_PER_GROUP)
        blocks = pl.ds(g * EXPERTS_PER_GROUP, EXPERTS_PER_GROUP)
        jobs.append((wg_hbm.at[layer, blocks], stage_in_ref,
                     [wg_ref.at[:, expert_cols(e)] for e in experts]))
        jobs.append((wu_hbm.at[layer, blocks], stage_in_ref,
                     [wu_ref.at[:, expert_cols(e)] for e in experts]))
        jobs.append((wd_hbm.at[layer, blocks], stage_out_ref,
                     [wd_ref.at[expert_cols(e), :] for e in experts]))

    def copy(i):
        src, stage_ref, _ = jobs[i]
        return pltpu.make_async_copy(src, stage_ref.at[i % 2], sem.at[i % 2])

    copy(0).start()
    for i, (_, stage_ref, dsts) in enumerate(jobs):
        if i + 1 < len(jobs):
            copy(i + 1).start()
        copy(i).wait()
        for j, dst in enumerate(dsts):
            dst[...] = stage_ref[i % 2, j].astype(BF16)


def _moe_kernel(layer, x_ref, wr_ref, rb_ref, prefix_ref, wg_hbm, wu_hbm, wd_hbm, g_ref, b_ref,
                o_ref, rows_ref, xs_ref, ws_ref, ys_ref, unsort_ref, wg_ref, wu_ref, wd_ref,
                stage_in_ref, stage_out_ref, sem):
    @pl.when(pl.program_id(0) == 0)
    def _():
        _load_expert_weights(layer, wg_hbm, wu_hbm, wd_hbm, wg_ref, wu_ref, wd_ref,
                             stage_in_ref, stage_out_ref, sem)

    slabs = range(x_ref.shape[0] // TOKEN_TILE)
    group_cols = EXPERTS_PER_GROUP * D_FF_EXPERT

    def piece(slab, starts, g, c):
        cols_g = slice(g * group_cols, (g + 1) * group_cols)
        r0 = pl.multiple_of(starts[g] // SORT_ALIGN * SORT_ALIGN + c * SORT_CHUNK, SORT_ALIGN)
        xs = xs_ref[slab, pl.ds(r0, SORT_CHUNK), :]
        gate = jnp.dot(xs, wg_ref[:, cols_g], preferred_element_type=F32)
        up = jnp.dot(xs, wu_ref[:, cols_g], preferred_element_type=F32)
        wc = ws_ref[slab, pl.ds(r0, SORT_CHUNK), :]
        w_cols = jnp.concatenate(
            [jnp.broadcast_to(wc[:, e:e + 1], (SORT_CHUNK, D_FF_EXPERT))
             for e in range(g * EXPERTS_PER_GROUP, (g + 1) * EXPERTS_PER_GROUP)], axis=1)
        hidden = (jax.nn.silu(gate) * up * w_cols).astype(BF16)
        ys_ref[slab, pl.ds(r0, SORT_CHUNK), :] += jnp.dot(hidden, wd_ref[cols_g, :],
                                                           preferred_element_type=F32)

    layout = []
    for slab in slabs:
        counts, starts = _moe_route_and_sort(slab, x_ref, wr_ref, rb_ref, prefix_ref,
                                             rows_ref, xs_ref, ws_ref, ys_ref, unsort_ref)
        for g in range(N_EXPERT_GROUPS):
            piece(slab, starts, g, 0)
        layout.append((counts, starts))

    for slab in slabs:
        counts, starts = layout[slab]
        for g in range(N_EXPERT_GROUPS):
            span = counts[g] + starts[g] % SORT_ALIGN
            lax.fori_loop(1, (span + (SORT_CHUNK - 1)) // SORT_CHUNK,
                          lambda c, carry, slab=slab, starts=starts, g=g:
                          (piece(slab, starts, g, c), carry)[1], 0)

    for slab in slabs:
        rows = slice(slab * TOKEN_TILE, (slab + 1) * TOKEN_TILE)
        y = jnp.dot(unsort_ref[slab], ys_ref[slab, :TOKEN_TILE, :].astype(BF16),
                    preferred_element_type=F32)
        o_ref[rows, :] = _layer_norm(DEEPNORM_ALPHA * x_ref[rows, :] + y, g_ref[...], b_ref[...])


def _moe_route_and_sort(slab, x_ref, wr_ref, rb_ref, prefix_ref,
                        rows_ref, xs_ref, ws_ref, ys_ref, unsort_ref):
    n_tok = TOKEN_TILE
    rows_ref = rows_ref.at[slab]
    x = x_ref[slab * TOKEN_TILE:(slab + 1) * TOKEN_TILE, :]
    xb = x.astype(BF16)
    x_lo = (x - xb.astype(F32)).astype(BF16)
    r = jnp.dot(xb, wr_ref[...], preferred_element_type=F32)
    logits = (r[:, :LANES] + r[:, LANES:]
              + jnp.dot(x_lo, wr_ref[:, :LANES], preferred_element_type=F32))

    scores_t = jax.nn.sigmoid(logits.T[:N_EXPERTS])
    rows_ref[:N_EXPERTS, :] = scores_t
    rows_ref[N_EXPERTS:2 * N_EXPERTS, :] = scores_t + rb_ref[...]
    score_rows = [rows_ref[e:e + 1, :] for e in range(N_EXPERTS)]
    sel_rows = [rows_ref[N_EXPERTS + e:N_EXPERTS + e + 1, :] for e in range(N_EXPERTS)]
    best = None
    for g in range(N_EXPERT_GROUPS):
        members = slice(g * EXPERTS_PER_GROUP, (g + 1) * EXPERTS_PER_GROUP)
        cand = _top2_of_group(sel_rows[members], score_rows[members])
        cand = cand + (jnp.full(cand[1].shape, g, jnp.int32),)
        if best is None:
            best = cand
        else:
            better = cand[0] > best[0]
            best = tuple(jnp.where(better, c, o) for c, o in zip(cand, best))
    _, i1, i2, s1, s2, group = best
    denom = s1 + s2
    w1, w2 = s1 / denom, s2 / denom
    dense_rows = [
        jnp.where(group == e // EXPERTS_PER_GROUP,
                  jnp.where(i1 == e % EXPERTS_PER_GROUP, w1, 0.0)
                  + jnp.where(i2 == e % EXPERTS_PER_GROUP, w2, 0.0), 0.0)
        for e in range(N_EXPERTS)]

    in_group = [jnp.where(group == g, 1.0, 0.0) for g in range(N_EXPERT_GROUPS)]
    rows_ref[...] = jnp.zeros_like(rows_ref)
    for g in range(N_EXPERT_GROUPS):
        rows_ref[g:g + 1, :] = in_group[g]
    onehot_t = rows_ref[:2 * SUBLANES, :].astype(BF16)
    earlier = jnp.dot(onehot_t, prefix_ref[...], preferred_element_type=F32)
    counts = [jnp.sum(m).astype(jnp.int32) for m in in_group]
    starts, nxt = [], jnp.int32(0)
    for g in range(N_EXPERT_GROUPS):
        starts.append(nxt)
        nxt = nxt + counts[g]
    dest_t = sum(in_group[g] * (earlier[g:g + 1] + starts[g].astype(F32))
                 for g in range(N_EXPERT_GROUPS))

    for e in range(N_EXPERTS):
        rows_ref[e:e + 1, :] = dense_rows[e]
    rows_ref[N_EXPERTS:N_EXPERTS + 1, :] = dest_t
    cols = rows_ref[...].T
    dest_col = cols[:, N_EXPERTS:N_EXPERTS + 1].astype(jnp.int32)
    to_sorted = jnp.where(
        lax.broadcasted_iota(jnp.int32, (n_tok, n_tok), 0) == dest_t.astype(jnp.int32),
        1.0, 0.0).astype(BF16)
    unsort_ref[slab] = jnp.where(
        lax.broadcasted_iota(jnp.int32, (n_tok, n_tok), 1) == dest_col, 1.0, 0.0).astype(BF16)

    xs_ref[slab, :n_tok, :] = jnp.dot(to_sorted, xb, preferred_element_type=F32).astype(BF16)
    xs_ref[slab, n_tok:, :] = jnp.zeros((SORT_CHUNK, D_MODEL), BF16)
    w_hi = cols.astype(BF16)
    w_split = jnp.concatenate([w_hi, (cols - w_hi.astype(F32)).astype(BF16)], axis=1)
    w_sorted = jnp.dot(to_sorted, w_split, preferred_element_type=F32)
    ws_ref[slab, :n_tok, :] = w_sorted[:, :LANES] + w_sorted[:, LANES:]
    ws_ref[slab, n_tok:, :] = jnp.zeros((SORT_CHUNK, LANES), F32)
    ys_ref[slab] = jnp.zeros(ys_ref.shape[1:], F32)
    return counts, starts


def _moe_call(x, wr_split, rb_col, prefix_op, w_gate, w_up, w_down, layer, ln_g, ln_b):
    t = x.shape[0]
    row = pl.BlockSpec((CHANNEL_TILE, D_MODEL), lambda i: (i, 0))
    full = lambda shape: pl.BlockSpec(shape, lambda i: (0,) * len(shape))
    in_hbm = pl.BlockSpec(memory_space=pl.ANY)
    n_slabs = CHANNEL_TILE // TOKEN_TILE
    return pl.pallas_call(
        functools.partial(_moe_kernel, layer),
        grid=(t // CHANNEL_TILE,),
        in_specs=[row,
                  full((D_MODEL, 2 * LANES)), full((N_EXPERTS, 1)), full((TOKEN_TILE, TOKEN_TILE)),
                  in_hbm, in_hbm, in_hbm,
                  full((1, D_MODEL)), full((1, D_MODEL))],
        out_specs=row,
        out_shape=jax.ShapeDtypeStruct((t, D_MODEL), F32),
        scratch_shapes=[pltpu.VMEM((n_slabs, LANES, TOKEN_TILE), F32),
                        pltpu.VMEM((n_slabs, TOKEN_TILE + SORT_CHUNK, D_MODEL), BF16),
                        pltpu.VMEM((n_slabs, TOKEN_TILE + SORT_CHUNK, LANES), F32),
                        pltpu.VMEM((n_slabs, TOKEN_TILE + SORT_CHUNK, D_MODEL), F32),
                        pltpu.VMEM((n_slabs, TOKEN_TILE, TOKEN_TILE), BF16),
                        pltpu.VMEM((D_MODEL, FF_ALL), BF16),
                        pltpu.VMEM((D_MODEL, FF_ALL), BF16),
                        pltpu.VMEM((FF_ALL, D_MODEL), BF16),
                        pltpu.VMEM((2, EXPERTS_PER_GROUP, D_MODEL, D_FF_EXPERT), F32),
                        pltpu.VMEM((2, EXPERTS_PER_GROUP, D_FF_EXPERT, D_MODEL), F32),
                        pltpu.SemaphoreType.DMA((2,))],
        compiler_params=_params(("arbitrary",)),
        name="moe",
    )(x, wr_split, rb_col, prefix_op, w_gate, w_up, w_down, ln_g, ln_b)


def kernel(x, ln_in_g, ln_in_b, w_in, b_gate, conv_w, sg_w, sg_b, sg_ln_g, sg_ln_b, w_br, w_o,
           ln_mix_g, ln_mix_b, w_router, router_bias, w_gate, w_up, w_down, ln_ffn_g, ln_ffn_b):
    batch, seq_len, d = x.shape
    assert d == D_MODEL and seq_len % TOKEN_TILE == 0 and seq_len % (Q_SUBTILES * Q_TILE) == 0
    t = batch * seq_len
    vec = lambda a: a.reshape(1, -1)

    wr_pad = jnp.pad(w_router, ((0, 0), (0, LANES - N_EXPERTS)))
    wr_hi = wr_pad.astype(BF16)
    wr_split = jnp.concatenate([wr_hi, (wr_pad - wr_hi.astype(F32)).astype(BF16)], axis=1)
    rb_col = router_bias.reshape(N_EXPERTS, 1)
    prefix_op = (lax.broadcasted_iota(jnp.int32, (TOKEN_TILE, TOKEN_TILE), 0)
                 < lax.broadcasted_iota(jnp.int32, (TOKEN_TILE, TOKEN_TILE), 1)).astype(BF16)

    h = x.reshape(t, d)
    for l in range(DEPTH):
        sg_b_full = jnp.repeat(sg_b[l].T, GROUP_DIM_C, axis=1)
        mixer_args = (w_in, l, conv_w[l], sg_w[l], sg_b_full, vec(sg_ln_g[l]), vec(sg_ln_b[l]),
                      seq_len)
        if l == 0:
            h, y_a, y_c, q, k, v = _mixer_in_call(h, *mixer_args,
                                                  input_ln=(vec(ln_in_g), vec(ln_in_b)))
        else:
            y_a, y_c, q, k, v = _mixer_in_call(h, *mixer_args)
        shape3 = (batch, seq_len, BRANCH_W)
        y_b = _attn_call(q.reshape(shape3), k.reshape(shape3), v.reshape(shape3)).reshape(t, BRANCH_W)
        h = _merge_call(h, y_a, y_b, y_c, w_in, w_br, w_o, l, b_gate[l],
                        vec(ln_mix_g[l]), vec(ln_mix_b[l]))
        h = _moe_call(h, wr_split, rb_col, prefix_op, w_gate, w_up, w_down, l,
                      vec(ln_ffn_g[l]), vec(ln_ffn_b[l]))
    return h.reshape(batch, seq_len, d)
```

```python
import functools

import jax
import jax.numpy as jnp
from jax import lax
from jax.experimental import pallas as pl
from jax.experimental.pallas import tpu as pltpu

D_MODEL = 1024
DEPTH = 4
BRANCH_W = 256
N_HEADS = 4
HEAD_DIM = 64
CHUNK = 128
N_GROUPS_C = 4
GROUP_DIM_C = 64
MIX_COLS = 8 * BRANCH_W
N_BRANCH = 3
N_EXPERTS = 16
EXPERTS_PER_GROUP = 4
N_EXPERT_GROUPS = 4
D_FF_EXPERT = 128
FF_ALL = N_EXPERTS * D_FF_EXPERT
DEEPNORM_ALPHA = (2 * DEPTH) ** 0.25
LN_EPS = 1e-5

LANES = 128
SUBLANES = 8
VMEM_LIMIT = 56 * 1024 * 1024

TOKEN_TILE = 512
CHANNEL_TILE = 1024
SORT_ALIGN = 16
SORT_CHUNK = 10 * SORT_ALIGN
Q_TILE = 128
K_TILE = 128
Q_SUBTILES = 16
assert Q_TILE == K_TILE
EXIT_BITS = 127.0
NO_TILE_BITS = 16384.0
LOG2_E = 1.4426950408889634

F32 = jnp.float32
BF16 = jnp.bfloat16


def _layer_norm(x, g, b):
    mu = jnp.mean(x, axis=-1, keepdims=True)
    xc = x - mu
    var = jnp.mean(xc * xc, axis=-1, keepdims=True)
    return xc * lax.rsqrt(var + LN_EPS) * g + b


def _slabs(n_rows):
    return [slice(r, r + TOKEN_TILE) for r in range(0, n_rows, TOKEN_TILE)]


def _params(sem):
    return pltpu.CompilerParams(dimension_semantics=sem, vmem_limit_bytes=VMEM_LIMIT)


def _shift_rows(u, prev, shift):
    rolled = pltpu.roll(u, shift, axis=0)
    head_rows = lax.broadcasted_iota(jnp.int32, (SUBLANES, u.shape[1]), 0)
    head = jnp.where(head_rows < shift, pltpu.roll(prev, shift, axis=0), rolled[:SUBLANES])
    return jnp.concatenate([head, rolled[SUBLANES:]], axis=0)


def _mixer_in_kernel(tiles_per_seq, normalize_input, *refs):
    if normalize_input:
        (h_ref, ing_ref, inb_ref, w32_ref, convw_ref, sgw_ref, sgb_ref, lng_ref, lnb_ref,
         h_out_ref, ya_ref, yc_ref, q_ref, k_ref, v_ref, tail_ref, w_ref) = refs
    else:
        (h_ref, w32_ref, convw_ref, sgw_ref, sgb_ref, lng_ref, lnb_ref,
         ya_ref, yc_ref, q_ref, k_ref, v_ref, tail_ref, w_ref) = refs

    @pl.when(pl.program_id(0) == 0)
    def _():
        w_ref[...] = w32_ref[0].astype(BF16)

    @pl.when(pl.program_id(0) % tiles_per_seq == 0)
    def _():
        tail_ref[...] = jnp.zeros_like(tail_ref)

    col = lambda p, j: p[:, j * BRANCH_W:(j + 1) * BRANCH_W]
    t_idx = lax.broadcasted_iota(jnp.int32, (CHUNK, CHUNK), 0)
    s_idx = lax.broadcasted_iota(jnp.int32, (CHUNK, CHUNK), 1)
    w_cat = jnp.concatenate(
        [jnp.where(s_idx <= t_idx, sgw_ref[g], 0.0).astype(BF16) for g in range(N_GROUPS_C)], axis=1)
    lane_group = lax.broadcasted_iota(jnp.int32, (1, BRANCH_W), 1) // GROUP_DIM_C

    prev = tail_ref[...]
    for base in range(0, h_ref.shape[0], TOKEN_TILE):
        rows = slice(base, base + TOKEN_TILE)
        h = h_ref[rows, :]
        if normalize_input:
            h = _layer_norm(h, ing_ref[...], inb_ref[...])
            h_out_ref[rows, :] = h
        hb = h.astype(BF16)
        proj_c = jnp.dot(hb, w_ref[:, 6 * BRANCH_W:], preferred_element_type=F32)
        proj_a = jnp.dot(hb, w_ref[:, :3 * BRANCH_W], preferred_element_type=F32)
        proj_b = jnp.dot(hb, w_ref[:, 3 * BRANCH_W:6 * BRANCH_W], preferred_element_type=F32)

        gate_u = jax.nn.gelu(col(proj_c, 0))
        vn = _layer_norm(jax.nn.gelu(col(proj_c, 1)), lng_ref[...], lnb_ref[...]).astype(BF16)
        for c in range(TOKEN_TILE // CHUNK):
            chunk = slice(c * CHUNK, (c + 1) * CHUNK)
            v_chunk = vn[chunk]
            v_groups = jnp.concatenate(
                [jnp.where(lane_group == g, v_chunk, jnp.zeros_like(v_chunk))
                 for g in range(N_GROUPS_C)], axis=0)
            f = jnp.dot(w_cat, v_groups, preferred_element_type=F32) + sgb_ref[...]
            yc_ref[base + c * CHUNK:base + (c + 1) * CHUNK, :] = (gate_u[chunk] * f).astype(BF16)

        u = col(proj_a, 2) * col(proj_a, 0)
        conv = (convw_ref[0:1, :] * _shift_rows(u, prev, 2)
                + convw_ref[1:2, :] * _shift_rows(u, prev, 1)
                + convw_ref[2:3, :] * u)
        prev = u[TOKEN_TILE - SUBLANES:]
        ya_ref[rows, :] = (col(proj_a, 1) * conv).astype(BF16)

        q_ref[rows, :] = (col(proj_b, 0) * (HEAD_DIM ** -0.5 * LOG2_E)).astype(BF16)
        k_ref[rows, :] = col(proj_b, 1).astype(BF16)
        v_ref[rows, :] = col(proj_b, 2).astype(BF16)
    tail_ref[...] = prev


def _mixer_in_call(h, w_in, layer, conv_w, sg_w, sg_b_full, sg_ln_g, sg_ln_b, seq_len,
                   input_ln=None):
    t = h.shape[0]
    row = pl.BlockSpec((CHANNEL_TILE, D_MODEL), lambda i: (i, 0))
    row_out = pl.BlockSpec((CHANNEL_TILE, BRANCH_W), lambda i: (i, 0))
    full = lambda shape: pl.BlockSpec(shape, lambda i: (0,) * len(shape))
    out_sds = jax.ShapeDtypeStruct((t, BRANCH_W), BF16)
    normalize = input_ln is not None
    ln_args = list(input_ln) if normalize else []
    return pl.pallas_call(
        functools.partial(_mixer_in_kernel, seq_len // CHANNEL_TILE, normalize),
        grid=(t // CHANNEL_TILE,),
        in_specs=[row] + [full((1, D_MODEL))] * len(ln_args) + [
            pl.BlockSpec((1, D_MODEL, MIX_COLS), lambda i: (layer, 0, 0),
                         pipeline_mode=pl.Buffered(1)),
            full((3, BRANCH_W)),
            full((N_GROUPS_C, CHUNK, CHUNK)),
            full((CHUNK, BRANCH_W)),
            full((1, BRANCH_W)),
            full((1, BRANCH_W)),
        ],
        out_specs=[row] * normalize + [row_out] * 5,
        out_shape=[jax.ShapeDtypeStruct((t, D_MODEL), F32)] * normalize + [out_sds] * 5,
        scratch_shapes=[pltpu.VMEM((SUBLANES, BRANCH_W), F32),
                        pltpu.VMEM((D_MODEL, MIX_COLS), BF16)],
        compiler_params=_params(("arbitrary",)),
        name="mixer_in",
    )(h, *ln_args, w_in, conv_w, sg_w, sg_b_full, sg_ln_g, sg_ln_b)


def _attn_kernel(q_ref, k_ref, v_ref, o_ref, acc_ref, spent_ref):
    lane_head = lax.broadcasted_iota(jnp.int32, (1, BRANCH_W), 1) // HEAD_DIM
    t_idx = lax.broadcasted_iota(jnp.int32, (N_HEADS * Q_TILE, K_TILE), 0) % Q_TILE
    s_idx = lax.broadcasted_iota(jnp.int32, (N_HEADS * Q_TILE, K_TILE), 1)
    below_diagonal = s_idx < t_idx
    j_op = lax.broadcasted_iota(jnp.int32, (2 * K_TILE, 2 * K_TILE), 0) % K_TILE
    s_op = lax.broadcasted_iota(jnp.int32, (2 * K_TILE, 2 * K_TILE), 1)
    suffix_op = jnp.where((s_op >= K_TILE) | (j_op > s_op), 1.0, 0.0).astype(BF16)

    def visit(q_stack, kb, spent, diagonal):
        start = pl.multiple_of(jnp.maximum(kb, 0) * K_TILE, K_TILE)
        k_blk = k_ref[0, pl.ds(start, K_TILE), :]
        v_blk = v_ref[0, pl.ds(start, K_TILE), :]
        z = lax.dot_general(q_stack, k_blk, (((1,), (1,)), ((), ())), preferred_element_type=F32)
        softplus = jnp.maximum(z, 0.0) + jnp.log2(1.0 + jnp.exp2(-jnp.abs(z)))
        if diagonal:
            softplus = jnp.where(below_diagonal, softplus, 0.0)
        else:
            spent = spent + jnp.where(kb >= 0, 0.0, NO_TILE_BITS)
        hi = softplus.astype(BF16)
        lo = (softplus - hi.astype(F32)).astype(BF16)
        sums = jnp.dot(jnp.concatenate([hi, lo], axis=1), suffix_op, preferred_element_type=F32)
        a = jnp.exp2(z - softplus - (sums[:, :K_TILE] + spent))
        if diagonal:
            a = jnp.where(below_diagonal, a, 0.0)
        a = a.astype(BF16)
        a_heads = jnp.concatenate([a[h * Q_TILE:(h + 1) * Q_TILE] for h in range(N_HEADS)], axis=1)
        v_heads = jnp.concatenate(
            [jnp.where(lane_head == h, v_blk, jnp.zeros_like(v_blk)) for h in range(N_HEADS)], axis=0)
        out = jnp.dot(a_heads, v_heads, preferred_element_type=F32)
        return spent + sums[:, K_TILE:], out

    def stacked_queries(sub):
        q = q_ref[0, sub * Q_TILE:(sub + 1) * Q_TILE, :]
        return jnp.concatenate(
            [jnp.where(lane_head == h, q, jnp.zeros_like(q)) for h in range(N_HEADS)], axis=0)

    first_tile = pl.program_id(1) * Q_SUBTILES
    least = []
    for sub in range(Q_SUBTILES):
        tile = first_tile + sub
        q_stack = stacked_queries(sub)
        spent, out0 = visit(q_stack, tile, jnp.zeros((N_HEADS * Q_TILE, K_TILE), F32), True)
        spent, out1 = visit(q_stack, tile - 1, spent, False)
        acc_ref[sub] = out0 + out1
        spent_ref[sub] = spent
        least.append(jnp.min(spent))

    def walk(state):
        step, least = state

        def advance(sub):
            spent, out = visit(stacked_queries(sub), first_tile + sub - step, spent_ref[sub], False)
            acc_ref[sub] += out
            spent_ref[sub] = spent
            return jnp.min(spent)

        return step + 1, tuple(
            lax.cond(least[sub] < EXIT_BITS, functools.partial(advance, sub), lambda sub=sub: least[sub])
            for sub in range(Q_SUBTILES))

    def unfinished(state):
        _, least = state
        return functools.reduce(jnp.minimum, least) < EXIT_BITS

    lax.while_loop(unfinished, walk, (jnp.int32(2), tuple(least)))
    for sub in range(Q_SUBTILES):
        o_ref[0, sub * Q_TILE:(sub + 1) * Q_TILE, :] = acc_ref[sub].astype(BF16)


def _attn_call(q, k, v):
    b, s, _ = q.shape
    rows = Q_SUBTILES * Q_TILE
    tile = pl.BlockSpec((1, rows, BRANCH_W), lambda bi, qi: (bi, qi, 0))
    seq = pl.BlockSpec((1, s, BRANCH_W), lambda bi, qi: (bi, 0, 0))
    return pl.pallas_call(
        _attn_kernel,
        grid=(b, s // rows),
        in_specs=[tile, seq, seq],
        out_specs=tile,
        out_shape=jax.ShapeDtypeStruct((b, s, BRANCH_W), BF16),
        scratch_shapes=[pltpu.VMEM((Q_SUBTILES, Q_TILE, BRANCH_W), F32),
                        pltpu.VMEM((Q_SUBTILES, N_HEADS * Q_TILE, K_TILE), F32)],
        compiler_params=_params(("parallel", "parallel")),
        name="attn",
    )(q, k, v)


def _merge_kernel(h_ref, ya_ref, yb_ref, yc_ref, wga32_ref, wgb32_ref, wgc32_ref, bgate_ref,
                  wbr32_ref, wo32_ref, g_ref, b_ref, o_ref, wg_ref, wbr_ref, wo_ref):
    @pl.when(pl.program_id(0) == 0)
    def _():
        for i, w32_ref in enumerate((wga32_ref, wgb32_ref, wgc32_ref)):
            wg_ref[:, i * D_MODEL:(i + 1) * D_MODEL] = w32_ref[0].astype(BF16)
        wbr_ref[...] = wbr32_ref[0].astype(BF16)
        wo_ref[...] = wo32_ref[0].astype(BF16)

    for rows in _slabs(h_ref.shape[0]):
        h = h_ref[rows, :]
        logits = jnp.dot(h.astype(BF16), wg_ref[...], preferred_element_type=F32)
        merged = None
        for i, y_ref in enumerate((ya_ref, yb_ref, yc_ref)):
            gate = 0.5 + 0.5 * jnp.tanh(
                0.5 * (logits[:, i * D_MODEL:(i + 1) * D_MODEL] + bgate_ref[i:i + 1, :]))
            term = gate * jnp.dot(y_ref[rows, :], wbr_ref[i], preferred_element_type=F32)
            merged = term if merged is None else merged + term
        mix = jnp.dot(merged.astype(BF16), wo_ref[...], preferred_element_type=F32)
        o_ref[rows, :] = _layer_norm(DEEPNORM_ALPHA * h + mix, g_ref[...], b_ref[...])


def _merge_call(h, y_a, y_b, y_c, w_in, w_br, w_o, layer, b_gate, ln_g, ln_b):
    t = h.shape[0]
    row = pl.BlockSpec((CHANNEL_TILE, D_MODEL), lambda i: (i, 0))
    branch = pl.BlockSpec((CHANNEL_TILE, BRANCH_W), lambda i: (i, 0))
    full = lambda shape: pl.BlockSpec(shape, lambda i: (0,) * len(shape))
    resident = lambda shape, index: pl.BlockSpec(shape, index, pipeline_mode=pl.Buffered(1))
    gate_cols = [resident((1, D_MODEL, D_MODEL), lambda i, j=j: (layer, 0, MIX_COLS // D_MODEL + j))
                 for j in range(N_BRANCH)]
    return pl.pallas_call(
        _merge_kernel,
        grid=(t // CHANNEL_TILE,),
        in_specs=[row, branch, branch, branch, *gate_cols,
                  full((N_BRANCH, D_MODEL)),
                  resident((1, N_BRANCH, BRANCH_W, D_MODEL), lambda i: (layer, 0, 0, 0)),
                  resident((1, D_MODEL, D_MODEL), lambda i: (layer, 0, 0)),
                  full((1, D_MODEL)), full((1, D_MODEL))],
        out_specs=row,
        out_shape=jax.ShapeDtypeStruct((t, D_MODEL), F32),
        scratch_shapes=[pltpu.VMEM((D_MODEL, N_BRANCH * D_MODEL), BF16),
                        pltpu.VMEM((N_BRANCH, BRANCH_W, D_MODEL), BF16),
                        pltpu.VMEM((D_MODEL, D_MODEL), BF16)],
        compiler_params=_params(("arbitrary",)),
        name="merge",
    )(h, y_a, y_b, y_c, w_in, w_in, w_in, b_gate, w_br, w_o, ln_g, ln_b)


def _top2_of_group(sel, scores):
    def first_argmax(vals):
        top = functools.reduce(jnp.maximum, vals)
        pos = jnp.full(top.shape, len(vals) - 1, jnp.int32)
        for j in reversed(range(len(vals) - 1)):
            pos = jnp.where(vals[j] == top, j, pos)
        return top, pos

    def pick(rows, pos):
        out = rows[-1]
        for j in reversed(range(len(rows) - 1)):
            out = jnp.where(pos == j, rows[j], out)
        return out

    m1, i1 = first_argmax(sel)
    m2, i2 = first_argmax([jnp.where(i1 == j, -jnp.inf, v) for j, v in enumerate(sel)])
    return m1 + m2, i1, i2, pick(scores, i1), pick(scores, i2)


def _load_expert_weights(layer, wg_hbm, wu_hbm, wd_hbm, wg_ref, wu_ref, wd_ref,
                         stage_in_ref, stage_out_ref, sem):
    expert_cols = lambda e: slice(e * D_FF_EXPERT, (e + 1) * D_FF_EXPERT)
    jobs = []
    for g in range(N_EXPERT_GROUPS):
        experts = range(g * EXPERTS_PER_GROUP, (g + 1) * EXPERTS_PER_GROUP)
        blocks = pl.ds(g * EXPERTS_PER_GROUP, EXPERTS_PER_GROUP)
        jobs.append((wg_hbm.at[layer, blocks], stage_in_ref,
                     [wg_ref.at[:, expert_cols(e)] for e in experts]))
        jobs.append((wu_hbm.at[layer, blocks], stage_in_ref,
                     [wu_ref.at[:, expert_cols(e)] for e in experts]))
        jobs.append((wd_hbm.at[layer, blocks], stage_out_ref,
                     [wd_ref.at[expert_cols(e), :] for e in experts]))

    def copy(i):
        src, stage_ref, _ = jobs[i]
        return pltpu.make_async_copy(src, stage_ref.at[i % 2], sem.at[i % 2])

    copy(0).start()
    for i, (_, stage_ref, dsts) in enumerate(jobs):
        if i + 1 < len(jobs):
            copy(i + 1).start()
        copy(i).wait()
        for j, dst in enumerate(dsts):
            dst[...] = stage_ref[i % 2, j].astype(BF16)


def _moe_kernel(layer, x_ref, wr_ref, rb_ref, prefix_ref, wg_hbm, wu_hbm, wd_hbm, g_ref, b_ref,
                o_ref, rows_ref, xs_ref, ws_ref, ys_ref, unsort_ref, wg_ref, wu_ref, wd_ref,
                stage_in_ref, stage_out_ref, sem):
    @pl.when(pl.program_id(0) == 0)
    def _():
        _load_expert_weights(layer, wg_hbm, wu_hbm, wd_hbm, wg_ref, wu_ref, wd_ref,
                             stage_in_ref, stage_out_ref, sem)

    slabs = range(x_ref.shape[0] // TOKEN_TILE)
    group_cols = EXPERTS_PER_GROUP * D_FF_EXPERT

    def piece(slab, starts, g, c):
        cols_g = slice(g * group_cols, (g + 1) * group_cols)
        r0 = pl.multiple_of(starts[g] // SORT_ALIGN * SORT_ALIGN + c * SORT_CHUNK, SORT_ALIGN)
        xs = xs_ref[slab, pl.ds(r0, SORT_CHUNK), :]
        gate = jnp.dot(xs, wg_ref[:, cols_g], preferred_element_type=F32)
        up = jnp.dot(xs, wu_ref[:, cols_g], preferred_element_type=F32)
        wc = ws_ref[slab, pl.ds(r0, SORT_CHUNK), :]
        w_cols = jnp.concatenate(
            [jnp.broadcast_to(wc[:, e:e + 1], (SORT_CHUNK, D_FF_EXPERT))
             for e in range(g * EXPERTS_PER_GROUP, (g + 1) * EXPERTS_PER_GROUP)], axis=1)
        hidden = (jax.nn.silu(gate) * up * w_cols).astype(BF16)
        ys_ref[slab, pl.ds(r0, SORT_CHUNK), :] += jnp.dot(hidden, wd_ref[cols_g, :],
                                                           preferred_element_type=F32)

    layout = []
    for slab in slabs:
        counts, starts = _moe_route_and_sort(slab, x_ref, wr_ref, rb_ref, prefix_ref,
                                             rows_ref, xs_ref, ws_ref, ys_ref, unsort_ref)
        for g in range(N_EXPERT_GROUPS):
            piece(slab, starts, g, 0)
        layout.append((counts, starts))

    for slab in slabs:
        counts, starts = layout[slab]
        for g in range(N_EXPERT_GROUPS):
            span = counts[g] + starts[g] % SORT_ALIGN
            lax.fori_loop(1, (span + (SORT_CHUNK - 1)) // SORT_CHUNK,
                          lambda c, carry, slab=slab, starts=starts, g=g:
                          (piece(slab, starts, g, c), carry)[1], 0)

    for slab in slabs:
        rows = slice(slab * TOKEN_TILE, (slab + 1) * TOKEN_TILE)
        y = jnp.dot(unsort_ref[slab], ys_ref[slab, :TOKEN_TILE, :].astype(BF16),
                    preferred_element_type=F32)
        o_ref[rows, :] = _layer_norm(DEEPNORM_ALPHA * x_ref[rows, :] + y, g_ref[...], b_ref[...])


def _moe_route_and_sort(slab, x_ref, wr_ref, rb_ref, prefix_ref,
                        rows_ref, xs_ref, ws_ref, ys_ref, unsort_ref):
    n_tok = TOKEN_TILE
    rows_ref = rows_ref.at[slab]
    x = x_ref[slab * TOKEN_TILE:(slab + 1) * TOKEN_TILE, :]
    xb = x.astype(BF16)
    x_lo = (x - xb.astype(F32)).astype(BF16)
    r = jnp.dot(xb, wr_ref[...], preferred_element_type=F32)
    logits = (r[:, :LANES] + r[:, LANES:]
              + jnp.dot(x_lo, wr_ref[:, :LANES], preferred_element_type=F32))

    scores_t = jax.nn.sigmoid(logits.T[:N_EXPERTS])
    rows_ref[:N_EXPERTS, :] = scores_t
    rows_ref[N_EXPERTS:2 * N_EXPERTS, :] = scores_t + rb_ref[...]
    score_rows = [rows_ref[e:e + 1, :] for e in range(N_EXPERTS)]
    sel_rows = [rows_ref[N_EXPERTS + e:N_EXPERTS + e + 1, :] for e in range(N_EXPERTS)]
    best = None
    for g in range(N_EXPERT_GROUPS):
        members = slice(g * EXPERTS_PER_GROUP, (g + 1) * EXPERTS_PER_GROUP)
        cand = _top2_of_group(sel_rows[members], score_rows[members])
        cand = cand + (jnp.full(cand[1].shape, g, jnp.int32),)
        if best is None:
            best = cand
        else:
            better = cand[0] > best[0]
            best = tuple(jnp.where(better, c, o) for c, o in zip(cand, best))
    _, i1, i2, s1, s2, group = best
    denom = s1 + s2
    w1, w2 = s1 / denom, s2 / denom
    dense_rows = [
        jnp.where(group == e // EXPERTS_PER_GROUP,
                  jnp.where(i1 == e % EXPERTS_PER_GROUP, w1, 0.0)
                  + jnp.where(i2 == e % EXPERTS_PER_GROUP, w2, 0.0), 0.0)
        for e in range(N_EXPERTS)]

    in_group = [jnp.where(group == g, 1.0, 0.0) for g in range(N_EXPERT_GROUPS)]
    rows_ref[...] = jnp.zeros_like(rows_ref)
    for g in range(N_EXPERT_GROUPS):
        rows_ref[g:g + 1, :] = in_group[g]
    onehot_t = rows_ref[:2 * SUBLANES, :].astype(BF16)
    earlier = jnp.dot(onehot_t, prefix_ref[...], preferred_element_type=F32)
    counts = [jnp.sum(m).astype(jnp.int32) for m in in_group]
    starts, nxt = [], jnp.int32(0)
    for g in range(N_EXPERT_GROUPS):
        starts.append(nxt)
        nxt = nxt + counts[g]
    dest_t = sum(in_group[g] * (earlier[g:g + 1] + starts[g].astype(F32))
                 for g in range(N_EXPERT_GROUPS))

    for e in range(N_EXPERTS):
        rows_ref[e:e + 1, :] = dense_rows[e]
    rows_ref[N_EXPERTS:N_EXPERTS + 1, :] = dest_t
    cols = rows_ref[...].T
    dest_col = cols[:, N_EXPERTS:N_EXPERTS + 1].astype(jnp.int32)
    to_sorted = jnp.where(
        lax.broadcasted_iota(jnp.int32, (n_tok, n_tok), 0) == dest_t.astype(jnp.int32),
        1.0, 0.0).astype(BF16)
    unsort_ref[slab] = jnp.where(
        lax.broadcasted_iota(jnp.int32, (n_tok, n_tok), 1) == dest_col, 1.0, 0.0).astype(BF16)

    xs_ref[slab, :n_tok, :] = jnp.dot(to_sorted, xb, preferred_element_type=F32).astype(BF16)
    xs_ref[slab, n_tok:, :] = jnp.zeros((SORT_CHUNK, D_MODEL), BF16)
    w_hi = cols.astype(BF16)
    w_split = jnp.concatenate([w_hi, (cols - w_hi.astype(F32)).astype(BF16)], axis=1)
    w_sorted = jnp.dot(to_sorted, w_split, preferred_element_type=F32)
    ws_ref[slab, :n_tok, :] = w_sorted[:, :LANES] + w_sorted[:, LANES:]
    ws_ref[slab, n_tok:, :] = jnp.zeros((SORT_CHUNK, LANES), F32)
    ys_ref[slab] = jnp.zeros(ys_ref.shape[1:], F32)
    return counts, starts


def _moe_call(x, wr_split, rb_col, prefix_op, w_gate, w_up, w_down, layer, ln_g, ln_b):
    t = x.shape[0]
    row = pl.BlockSpec((CHANNEL_TILE, D_MODEL), lambda i: (i, 0))
    full = lambda shape: pl.BlockSpec(shape, lambda i: (0,) * len(shape))
    in_hbm = pl.BlockSpec(memory_space=pl.ANY)
    n_slabs = CHANNEL_TILE // TOKEN_TILE
    return pl.pallas_call(
        functools.partial(_moe_kernel, layer),
        grid=(t // CHANNEL_TILE,),
        in_specs=[row,
                  full((D_MODEL, 2 * LANES)), full((N_EXPERTS, 1)), full((TOKEN_TILE, TOKEN_TILE)),
                  in_hbm, in_hbm, in_hbm,
                  full((1, D_MODEL)), full((1, D_MODEL))],
        out_specs=row,
        out_shape=jax.ShapeDtypeStruct((t, D_MODEL), F32),
        scratch_shapes=[pltpu.VMEM((n_slabs, LANES, TOKEN_TILE), F32),
                        pltpu.VMEM((n_slabs, TOKEN_TILE + SORT_CHUNK, D_MODEL), BF16),
                        pltpu.VMEM((n_slabs, TOKEN_TILE + SORT_CHUNK, LANES), F32),
                        pltpu.VMEM((n_slabs, TOKEN_TILE + SORT_CHUNK, D_MODEL), F32),
                        pltpu.VMEM((n_slabs, TOKEN_TILE, TOKEN_TILE), BF16),
                        pltpu.VMEM((D_MODEL, FF_ALL), BF16),
                        pltpu.VMEM((D_MODEL, FF_ALL), BF16),
                        pltpu.VMEM((FF_ALL, D_MODEL), BF16),
                        pltpu.VMEM((2, EXPERTS_PER_GROUP, D_MODEL, D_FF_EXPERT), F32),
                        pltpu.VMEM((2, EXPERTS_PER_GROUP, D_FF_EXPERT, D_MODEL), F32),
                        pltpu.SemaphoreType.DMA((2,))],
        compiler_params=_params(("arbitrary",)),
        name="moe",
    )(x, wr_split, rb_col, prefix_op, w_gate, w_up, w_down, ln_g, ln_b)


def kernel(x, ln_in_g, ln_in_b, w_in, b_gate, conv_w, sg_w, sg_b, sg_ln_g, sg_ln_b, w_br, w_o,
           ln_mix_g, ln_mix_b, w_router, router_bias, w_gate, w_up, w_down, ln_ffn_g, ln_ffn_b):
    batch, seq_len, d = x.shape
    assert d == D_MODEL and seq_len % CHANNEL_TILE == 0 and seq_len % (Q_SUBTILES * Q_TILE) == 0
    t = batch * seq_len
    vec = lambda a: a.reshape(1, -1)

    wr_pad = jnp.pad(w_router, ((0, 0), (0, LANES - N_EXPERTS)))
    wr_hi = wr_pad.astype(BF16)
    wr_split = jnp.concatenate([wr_hi, (wr_pad - wr_hi.astype(F32)).astype(BF16)], axis=1)
    rb_col = router_bias.reshape(N_EXPERTS, 1)
    prefix_op = (lax.broadcasted_iota(jnp.int32, (TOKEN_TILE, TOKEN_TILE), 0)
                 < lax.broadcasted_iota(jnp.int32, (TOKEN_TILE, TOKEN_TILE), 1)).astype(BF16)

    h = x.reshape(t, d)
    for l in range(DEPTH):
        sg_b_full = jnp.repeat(sg_b[l].T, GROUP_DIM_C, axis=1)
        mixer_args = (w_in, l, conv_w[l], sg_w[l], sg_b_full, vec(sg_ln_g[l]), vec(sg_ln_b[l]),
                      seq_len)
        if l == 0:
            h, y_a, y_c, q, k, v = _mixer_in_call(h, *mixer_args,
                                                  input_ln=(vec(ln_in_g), vec(ln_in_b)))
        else:
            y_a, y_c, q, k, v = _mixer_in_call(h, *mixer_args)
        shape3 = (batch, seq_len, BRANCH_W)
        y_b = _attn_call(q.reshape(shape3), k.reshape(shape3), v.reshape(shape3)).reshape(t, BRANCH_W)
        h = _merge_call(h, y_a, y_b, y_c, w_in, w_br, w_o, l, b_gate[l],
                        vec(ln_mix_g[l]), vec(ln_mix_b[l]))
        h = _moe_call(h, wr_split, rb_col, prefix_op, w_gate, w_up, w_down, l,
                      vec(ln_ffn_g[l]), vec(ln_ffn_b[l]))
    return h.reshape(batch, seq_len, d)
```
